```python
import math
import jax
import jax.numpy as jnp
from jax import lax
import numpy as np

D_MODEL = 2048
BATCH = 16
SEQ = 2048
DEPTH = 2

N_META = 16
ROPE_THETA = 10000.0
EPS = 1e-6
Q_BLOCK = 128
SPARSE_Q_BLOCK = 32
TOPK_MAX = 256

MLA_HEADS = 8
MLA_Q_RANK = 512
MLA_KV_RANK = 512
MLA_NOPE = 128
MLA_ROPE = 64
MLA_V = 128
MLA_QK = MLA_NOPE + MLA_ROPE

DSA_HEADS = 8
DSA_HEAD_DIM = 128
IDX_HEADS = 16
IDX_DIM = 64

DIFF_HEADS = 4
DIFF_HEAD_DIM = 128
DIFF_V = 2 * DIFF_HEAD_DIM

N_BRANCH = 3
BRANCH_WIDTH = 1024
D_FF = ((8 * D_MODEL + 3 * 256 - 1) // (3 * 256)) * 256

IN_PROJ_SIZES = (
    MLA_Q_RANK, MLA_KV_RANK, MLA_ROPE,
    DSA_HEADS * DSA_HEAD_DIM, DSA_HEADS * DSA_HEAD_DIM, DSA_HEADS * DSA_HEAD_DIM,
    IDX_HEADS * IDX_DIM, IDX_DIM, IDX_HEADS,
    DIFF_HEADS * 2 * DIFF_HEAD_DIM, DIFF_HEADS * 2 * DIFF_HEAD_DIM, DIFF_HEADS * DIFF_V,
    N_BRANCH * D_MODEL,
)
D_IN = sum(IN_PROJ_SIZES)

kernel_name = 'hybrid_mla_dsa_diffattn_gated_block'


def rms_norm(x, g):
    xf = x.astype(jnp.float32)
    y = xf * lax.rsqrt(jnp.mean(xf * xf, axis=-1, keepdims=True) + EPS)
    return (y * g.astype(jnp.float32)).astype(x.dtype)


def rope(x, pos):
    d = x.shape[-1]
    half = d // 2
    inv = ROPE_THETA ** (-jnp.arange(half, dtype=jnp.float32) * (2.0 / d))
    ang = pos.astype(jnp.float32)[:, None] * inv[None, :]
    ang = ang.reshape((ang.shape[0],) + (1,) * (x.ndim - 3) + (half,))
    cos, sin = jnp.cos(ang), jnp.sin(ang)
    xf = x.astype(jnp.float32)
    x1, x2 = xf[..., :half], xf[..., half:]
    return jnp.concatenate([x1 * cos - x2 * sin, x2 * cos + x1 * sin], axis=-1).astype(x.dtype)


def _split_cols(a, sizes):
    out, start = [], 0
    for s in sizes:
        out.append(a[..., start:start + s])
        start += s
    return out


def _causal_mask(i, block, lk):
    qpos = i * block + jnp.arange(block)
    return jnp.arange(lk)[None, :] <= qpos[:, None]


def sweep_query_blocks(fn, block, *qs):
    b, lp = qs[0].shape[:2]
    nb = lp // block
    xs = tuple(jnp.moveaxis(a.reshape((b, nb, block) + a.shape[2:]), 1, 0) for a in qs)
    out = lax.map(lambda args: fn(args[0], *args[1:]), (jnp.arange(nb),) + xs)
    out = jnp.moveaxis(out, 0, 1)
    return out.reshape((b, lp) + out.shape[3:])


def dense_causal_attention(q, k, v, scale):
    lk = k.shape[1]

    def blk(i, qb):
        s = jnp.einsum('bqhd,bkhd->bhqk', qb, k).astype(jnp.float32) * scale
        s = jnp.where(_causal_mask(i, Q_BLOCK, lk), s, -jnp.inf)
        p = jax.nn.softmax(s, axis=-1).astype(v.dtype)
        return jnp.einsum('bhqk,bkhd->bqhd', p, v)

    return sweep_query_blocks(blk, Q_BLOCK, q)


def mla_attention(c_q, c_kv, k_pe, q_norm_g, kv_norm_g, w_q_up, w_kv_up, qk_g, pos):
    b, lp, _ = c_q.shape
    q = (rms_norm(c_q, q_norm_g) @ w_q_up).reshape(b, lp, MLA_HEADS, MLA_QK)
    kv = (rms_norm(c_kv, kv_norm_g) @ w_kv_up).reshape(b, lp, MLA_HEADS, MLA_NOPE + MLA_V)
    k_nope, v = kv[..., :MLA_NOPE], kv[..., MLA_NOPE:]
    k = jnp.concatenate(
        [k_nope, jnp.broadcast_to(k_pe[:, :, None, :], (b, lp, MLA_HEADS, MLA_ROPE))], axis=-1)
    q = rms_norm(q, qk_g[0])
    k = rms_norm(k, qk_g[1])
    q = jnp.concatenate([q[..., :MLA_NOPE], rope(q[..., MLA_NOPE:], pos)], axis=-1)
    k = jnp.concatenate([k[..., :MLA_NOPE], rope(k[..., MLA_NOPE:], pos)], axis=-1)
    o = dense_causal_attention(q, k, v, MLA_QK ** -0.5)
    return o.reshape(b, lp, BRANCH_WIDTH)


def dsa_attention(q, k, v, iq, ik, iw, qk_g, pos, topk):
    b, lp, _ = q.shape
    q = rope(rms_norm(q.reshape(b, lp, DSA_HEADS, DSA_HEAD_DIM), qk_g[0]), pos)
    k = rope(rms_norm(k.reshape(b, lp, DSA_HEADS, DSA_HEAD_DIM), qk_g[1]), pos)
    v = v.reshape(b, lp, DSA_HEADS, DSA_HEAD_DIM)
    iq = rope(iq.reshape(b, lp, IDX_HEADS, IDX_DIM), pos)
    ik = rope(ik[:, :, None, :], pos)[:, :, 0, :]
    iw = iw.astype(jnp.float32) * (IDX_HEADS ** -0.5 * IDX_DIM ** -0.5)
    gather = jax.vmap(lambda src, idx: src[idx])

    def blk(i, qb, iqb, iwb):
        causal = _causal_mask(i, SPARSE_Q_BLOCK, lp)
        logits = jnp.einsum('bqhd,bsd->bqhs', iqb, ik).astype(jnp.float32)
        score = jnp.einsum('bqhs,bqh->bqs', jax.nn.relu(logits), iwb)
        score = jnp.where(causal, score, -jnp.inf)
        _, sel = lax.top_k(score, topk)
        qpos = i * SPARSE_Q_BLOCK + jnp.arange(SPARSE_Q_BLOCK)
        valid = sel <= qpos[None, :, None]
        k_sel = gather(k, sel)
        v_sel = gather(v, sel)
        s = jnp.einsum('bqhd,bqkhd->bhqk', qb, k_sel).astype(jnp.float32) * (DSA_HEAD_DIM ** -0.5)
        s = jnp.where(valid[:, None], s, -jnp.inf)
        p = jax.nn.softmax(s, axis=-1).astype(v.dtype)
        return jnp.einsum('bhqk,bqkhd->bqhd', p, v_sel)

    o = sweep_query_blocks(blk, SPARSE_Q_BLOCK, q, iq, iw)
    return o.reshape(b, lp, BRANCH_WIDTH)


def diff_attention(q, k, v, qk_g, lam_vecs, subln_g, pos, lambda_init):
    b, lp, _ = q.shape
    q = rope(rms_norm(q.reshape(b, lp, DIFF_HEADS, 2, DIFF_HEAD_DIM), qk_g[0]), pos)
    k = rope(rms_norm(k.reshape(b, lp, DIFF_HEADS, 2, DIFF_HEAD_DIM), qk_g[1]), pos)
    v = v.reshape(b, lp, DIFF_HEADS, DIFF_V)
    lv = lam_vecs.astype(jnp.float32)
    lam = jnp.exp(jnp.sum(lv[0] * lv[1])) - jnp.exp(jnp.sum(lv[2] * lv[3])) + lambda_init

    def blk(i, qb):
        s = jnp.einsum('bqhjd,bkhjd->bhjqk', qb, k).astype(jnp.float32) * (DIFF_HEAD_DIM ** -0.5)
        s = jnp.where(_causal_mask(i, Q_BLOCK, lp), s, -jnp.inf)
        p = jax.nn.softmax(s, axis=-1)
        a = (p[:, :, 0] - lam * p[:, :, 1]).astype(v.dtype)
        return jnp.einsum('bhqk,bkhd->bqhd', a, v)

    o = sweep_query_blocks(blk, Q_BLOCK, q)
    o = rms_norm(o, subln_g) * (1.0 - lambda_init)
    return o.reshape(b, lp, BRANCH_WIDTH)


def swiglu(xn, w_in, w_out):
    gu = xn @ w_in
    return (jax.nn.silu(gu[..., :D_FF]) * gu[..., D_FF:]) @ w_out


def setup_inputs(seed: int = 0) -> dict:
    key = jax.random.key(seed)
    ks = jax.random.split(key, 20)
    f32 = jnp.float32
    nrm = jax.random.normal

    def gain(k, shape):
        return 1.0 + 0.02 * nrm(k, shape, f32)

    return {
        'x': nrm(ks[0], (BATCH, SEQ, D_MODEL), f32),
        'meta': nrm(ks[1], (N_META, D_MODEL), f32),
        'ln1_g': gain(ks[2], (DEPTH, D_MODEL)),
        'w_in': nrm(ks[3], (DEPTH, D_MODEL, D_IN), f32) * D_MODEL ** -0.5,
        'b_gate': 0.02 * nrm(ks[4], (DEPTH, N_BRANCH * D_MODEL), f32),
        'mla_q_norm_g': gain(ks[5], (DEPTH, MLA_Q_RANK)),
        'mla_kv_norm_g': gain(ks[6], (DEPTH, MLA_KV_RANK)),
        'w_mla_q_up': nrm(ks[7], (DEPTH, MLA_Q_RANK, MLA_HEADS * MLA_QK), f32) * MLA_Q_RANK ** -0.5,
        'w_mla_kv_up': nrm(ks[8], (DEPTH, MLA_KV_RANK, MLA_HEADS * (MLA_NOPE + MLA_V)), f32) * MLA_KV_RANK ** -0.5,
        'mla_qk_g': gain(ks[9], (DEPTH, 2, MLA_QK)),
        'dsa_qk_g': gain(ks[10], (DEPTH, 2, DSA_HEAD_DIM)),
        'diff_qk_g': gain(ks[11], (DEPTH, 2, DIFF_HEAD_DIM)),
        'diff_lambda': 0.1 * nrm(ks[12], (DEPTH, 4, DIFF_HEAD_DIM), f32),
        'diff_subln_g': gain(ks[13], (DEPTH, DIFF_V)),
        'w_branch': nrm(ks[14], (DEPTH, N_BRANCH, BRANCH_WIDTH, D_MODEL), f32) * BRANCH_WIDTH ** -0.5,
        'w_o': nrm(ks[15], (DEPTH, D_MODEL, D_MODEL), f32) * D_MODEL ** -0.5,
        'ln2_g': gain(ks[16], (DEPTH, D_MODEL)),
        'w_ffn_in': nrm(ks[17], (DEPTH, D_MODEL, 2 * D_FF), f32) * D_MODEL ** -0.5,
        'w_ffn_out': nrm(ks[18], (DEPTH, D_FF, D_MODEL), f32) * D_FF ** -0.5,
    }


def reference(x, meta, ln1_g, w_in, b_gate, mla_q_norm_g, mla_kv_norm_g, w_mla_q_up, w_mla_kv_up,
              mla_qk_g, dsa_qk_g, diff_qk_g, diff_lambda, diff_subln_g, w_branch, w_o, ln2_g,
              w_ffn_in, w_ffn_out):
    b, s, d = x.shape
    topk = min(TOPK_MAX, s // 4)
    l_tot = N_META + s
    lp = -(-l_tot // Q_BLOCK) * Q_BLOCK
    pos = jnp.arange(lp)
    h = jnp.concatenate(
        [jnp.broadcast_to(meta[None].astype(x.dtype), (b, N_META, d)), x,
         jnp.zeros((b, lp - l_tot, d), x.dtype)], axis=1)
    for layer in range(DEPTH):
        lambda_init = 0.8 - 0.6 * math.exp(-0.3 * layer)
        xn = rms_norm(h, ln1_g[layer])
        (c_q, c_kv, k_pe, dq, dk, dv, iq, ik, iw, fq, fk, fv, g) = _split_cols(
            xn @ w_in[layer], IN_PROJ_SIZES)
        o_a = mla_attention(c_q, c_kv, k_pe, mla_q_norm_g[layer], mla_kv_norm_g[layer],
                            w_mla_q_up[layer], w_mla_kv_up[layer], mla_qk_g[layer], pos)
        o_b = dsa_attention(dq, dk, dv, iq, ik, iw, dsa_qk_g[layer], pos, topk)
        o_c = diff_attention(fq, fk, fv, diff_qk_g[layer], diff_lambda[layer],
                             diff_subln_g[layer], pos, lambda_init)
        gates = jax.nn.sigmoid((g + b_gate[layer]).astype(jnp.float32)).astype(h.dtype)
        gates = gates.reshape(b, lp, N_BRANCH, d)
        merged = (gates[:, :, 0] * (o_a @ w_branch[layer, 0])
                  + gates[:, :, 1] * (o_b @ w_branch[layer, 1])
                  + gates[:, :, 2] * (o_c @ w_branch[layer, 2]))
        h = h + merged @ w_o[layer]
        h = h + swiglu(rms_norm(h, ln2_g[layer]), w_ffn_in[layer], w_ffn_out[layer])
    return h[:, N_META:N_META + s]
```

```python
import functools
import math

import jax
import jax.numpy as jnp
from jax import lax
from jax.experimental import pallas as pl
from jax.experimental.pallas import tpu as pltpu

F32 = jnp.float32
BF16 = jnp.bfloat16

D_MODEL = 2048
N_META = 16
ROPE_THETA = 10000.0
EPS = 1e-6
TOPK_MAX = 256
MLA_HEADS = 8
MLA_RANK = 512
MLA_NOPE = 128
MLA_ROPE = 64
MLA_V = 128
MLA_QK = MLA_NOPE + MLA_ROPE
DSA_HEADS = 8
IDX_HEADS = 16
IDX_DIM = 64
DIFF_HEADS = 4
DIFF_V = 256
HEAD_DIM = 128
N_BRANCH = 3
BRANCH_WIDTH = 1024
D_FF = 5632

LANE = 128
V7X_VMEM_BYTES = 64 * 1024 * 1024
VMEM_COMPILER_RESERVE = 6 * 1024 * 1024

TQ = 256
TK = 256
MLA_HEAD_PAD = 256
NEG = -1e30
BISECT_STEPS = 22

COL_CQ, COL_CKV, COL_DQ, COL_DK, COL_DV = 0, 512, 1024, 2048, 3072
COL_FQ, COL_FK, COL_FV, COL_IQ, COL_G = 4096, 5120, 6144, 7168, 8192
COL_KPE, COL_IKW, D_IN_PACKED = 14336, 14464, 14592


def _vmem_limit(block_bytes, scratch_bytes=0):
    need = 2 * sum(block_bytes) + scratch_bytes + VMEM_COMPILER_RESERVE
    return int(min(need, V7X_VMEM_BYTES - 2 * 1024 * 1024))


def _nbytes(shape, dtype):
    return math.prod(shape) * jnp.dtype(dtype).itemsize


def _params(sem, block_bytes, scratch_bytes=0):
    return pltpu.CompilerParams(dimension_semantics=sem,
                                vmem_limit_bytes=_vmem_limit(block_bytes, scratch_bytes))


def _rms(x, g):
    ms = jnp.mean(x * x, axis=-1, keepdims=True)
    return x * lax.rsqrt(ms + EPS) * g


def _sigmoid(x):
    return 1.0 / (1.0 + jnp.exp(-x))


def _norm_matmul_kernel(x_ref, g_ref, w_ref, o_ref, xn_ref):
    @pl.when(pl.program_id(1) == 0)
    def _():
        xn_ref[...] = _rms(x_ref[...], g_ref[...]).astype(BF16)

    o_ref[...] = jnp.dot(xn_ref[...], w_ref[...], preferred_element_type=F32)


def _norm_matmul(x, g, w, tm, tn):
    t, d = x.shape
    n = w.shape[1]
    blocks = [_nbytes((tm, d), F32), _nbytes((d, tn), BF16), _nbytes((tm, tn), F32)]
    return pl.pallas_call(
        _norm_matmul_kernel,
        grid=(t // tm, n // tn),
        in_specs=[pl.BlockSpec((tm, d), lambda i, j: (i, 0)),
                  pl.BlockSpec((1, d), lambda i, j: (0, 0)),
                  pl.BlockSpec((d, tn), lambda i, j: (0, j))],
        out_specs=pl.BlockSpec((tm, tn), lambda i, j: (i, j)),
        out_shape=jax.ShapeDtypeStruct((t, n), F32),
        scratch_shapes=[pltpu.VMEM((tm, d), BF16)],
        compiler_params=_params(("parallel", "arbitrary"), blocks, _nbytes((tm, d), BF16)),
        name="inproj",
    )(x, g, w)


def _ffn_kernel(h_ref, g_ref, wab_ref, w2_ref, o_ref, xn_ref, acc_ref, *, tf):
    f = pl.program_id(1)

    @pl.when(f == 0)
    def _():
        xn_ref[...] = _rms(h_ref[...], g_ref[...]).astype(BF16)
        acc_ref[...] = jnp.zeros_like(acc_ref)

    gu = jnp.dot(xn_ref[...], wab_ref[...], preferred_element_type=F32)
    a = gu[:, :tf]
    act = (a * _sigmoid(a)) * gu[:, tf:]
    acc_ref[...] += jnp.dot(act.astype(BF16), w2_ref[...], preferred_element_type=F32)

    @pl.when(f == pl.num_programs(1) - 1)
    def _():
        o_ref[...] = h_ref[...] + acc_ref[...]


def _ffn(h, g, wab, w2, tm, tf):
    t, d = h.shape
    nf = w2.shape[0] // tf
    blocks = [_nbytes((tm, d), F32), _nbytes((d, 2 * tf), BF16), _nbytes((tf, d), BF16),
              _nbytes((tm, d), F32)]
    scratch = _nbytes((tm, d), BF16) + _nbytes((tm, d), F32)
    return pl.pallas_call(
        functools.partial(_ffn_kernel, tf=tf),
        grid=(t // tm, nf),
        in_specs=[pl.BlockSpec((tm, d), lambda i, f: (i, 0)),
                  pl.BlockSpec((1, d), lambda i, f: (0, 0)),
                  pl.BlockSpec((d, 2 * tf), lambda i, f: (0, f)),
                  pl.BlockSpec((tf, d), lambda i, f: (f, 0))],
        out_specs=pl.BlockSpec((tm, d), lambda i, f: (i, 0)),
        out_shape=jax.ShapeDtypeStruct((t, d), F32),
        scratch_shapes=[pltpu.VMEM((tm, d), BF16), pltpu.VMEM((tm, d), F32)],
        compiler_params=_params(("parallel", "arbitrary"), blocks, scratch),
        name="ffn",
    )(h, g, wab, w2)


def _merge_kernel(oa_ref, ob_ref, oc_ref, g_ref, bg_ref, w_ref, o_ref, acc_ref):
    k = pl.program_id(1)
    gate = _sigmoid(g_ref[...] + bg_ref[...])

    def contrib(o_branch_ref):
        return gate * jnp.dot(o_branch_ref[...], w_ref[0], preferred_element_type=F32)

    @pl.when(k == 0)
    def _():
        acc_ref[...] = contrib(oa_ref)

    @pl.when(k == 1)
    def _():
        acc_ref[...] += contrib(ob_ref)

    @pl.when(k == 2)
    def _():
        o_ref[...] = (acc_ref[...] + contrib(oc_ref)).astype(o_ref.dtype)


def _merge(o_a, o_b, o_c, y, b_gate, w_branch, tm):
    t = o_a.shape[0]
    d = D_MODEL
    gate_block0 = COL_G // d
    blocks = [3 * _nbytes((tm, BRANCH_WIDTH), BF16), _nbytes((tm, d), F32),
              _nbytes((BRANCH_WIDTH, d), BF16), _nbytes((tm, d), BF16)]
    o_spec = pl.BlockSpec((tm, BRANCH_WIDTH), lambda i, k: (i, 0))
    return pl.pallas_call(
        _merge_kernel,
        grid=(t // tm, N_BRANCH),
        in_specs=[o_spec, o_spec, o_spec,
                  pl.BlockSpec((tm, d), lambda i, k: (i, gate_block0 + k)),
                  pl.BlockSpec((1, d), lambda i, k: (0, k)),
                  pl.BlockSpec((1, BRANCH_WIDTH, d), lambda i, k: (k, 0, 0))],
        out_specs=pl.BlockSpec((tm, d), lambda i, k: (i, 0)),
        out_shape=jax.ShapeDtypeStruct((t, d), BF16),
        scratch_shapes=[pltpu.VMEM((tm, d), F32)],
        compiler_params=_params(("parallel", "arbitrary"), blocks, _nbytes((tm, d), F32)),
        name="merge",
    )(o_a, o_b, o_c, y, b_gate, w_branch)


def _residual_matmul_kernel(h_ref, a_ref, w_ref, o_ref):
    o_ref[...] = h_ref[...] + jnp.dot(a_ref[...], w_ref[...], preferred_element_type=F32)


def _residual_matmul(h, a, w, tm):
    t, d = h.shape
    kdim = a.shape[1]
    blocks = [_nbytes((tm, d), F32), _nbytes((tm, kdim), BF16), _nbytes((kdim, d), BF16),
              _nbytes((tm, d), F32)]
    return pl.pallas_call(
        _residual_matmul_kernel,
        grid=(t // tm,),
        in_specs=[pl.BlockSpec((tm, d), lambda i: (i, 0)),
                  pl.BlockSpec((tm, kdim), lambda i: (i, 0)),
                  pl.BlockSpec((kdim, d), lambda i: (0, 0))],
        out_specs=pl.BlockSpec((tm, d), lambda i: (i, 0)),
        out_shape=jax.ShapeDtypeStruct((t, d), F32),
        compiler_params=_params(("parallel",), blocks),
        name="wo",
    )(h, a, w)


ROPE_COS, ROPE_SIN, ROPE_C2, ROPE_SA2, ROPE_SB2, ROPE_C1, ROPE_SA1, ROPE_SB1 = (
    k * LANE for k in range(8))
ROPE_TABLE_WIDTH = 8 * LANE


def _tab(rope_ref, off):
    return rope_ref[:, off:off + LANE]


def _rope128(y, rope_ref):
    return y * _tab(rope_ref, ROPE_COS) + pltpu.roll(y, 64, 1) * _tab(rope_ref, ROPE_SIN)


def _rope64(y, rope_ref, c, sa, sb):
    return (y * _tab(rope_ref, c) + pltpu.roll(y, 96, 1) * _tab(rope_ref, sa)
            + pltpu.roll(y, 32, 1) * _tab(rope_ref, sb))


def _norm_rope_heads(x_ref, g, rope_ref, o_ref):
    for h in range(x_ref.shape[1] // HEAD_DIM):
        sl = slice(h * HEAD_DIM, (h + 1) * HEAD_DIM)
        o_ref[:, sl] = _rope128(_rms(x_ref[:, sl], g), rope_ref).astype(BF16)


def _prep_dsa_kernel(dq_ref, dk_ref, dv_ref, iq_ref, ikw_ref, rope_ref, g_ref,
                     qo_ref, ko_ref, vo_ref, iqo_ref, ikzo_ref, iwo_ref):
    _norm_rope_heads(dq_ref, g_ref[0:1, :], rope_ref, qo_ref)
    _norm_rope_heads(dk_ref, g_ref[1:2, :], rope_ref, ko_ref)
    vo_ref[...] = dv_ref[...].astype(BF16)
    for p in range(IDX_HEADS // 2):
        sl = slice(p * LANE, (p + 1) * LANE)
        iqo_ref[:, sl] = _rope64(iq_ref[:, sl], rope_ref, ROPE_C2, ROPE_SA2, ROPE_SB2).astype(BF16)
    x = ikw_ref[...]
    ik = _rope64(x, rope_ref, ROPE_C1, ROPE_SA1, ROPE_SB1)
    ikzo_ref[:, 0:LANE] = ik.astype(BF16)
    ikzo_ref[:, LANE:2 * LANE] = pltpu.roll(ik, 64, 1).astype(BF16)
    iwo_ref[...] = pltpu.roll(x, 64, 1) * (IDX_HEADS ** -0.5 * IDX_DIM ** -0.5)


def _prep_diff_kernel(fq_ref, fk_ref, fv_ref, rope_ref, g_ref, qo_ref, ko_ref, vo_ref):
    _norm_rope_heads(fq_ref, g_ref[0:1, :], rope_ref, qo_ref)
    _norm_rope_heads(fk_ref, g_ref[1:2, :], rope_ref, ko_ref)
    vo_ref[...] = fv_ref[...].astype(BF16)


def _prep_mla_kernel(cq_ref, ckv_ref, kpe_ref, rope_ref, qg_ref, kvg_ref, wq_ref, wkv_ref,
                     gqk_ref, qo_ref, ko_ref, vo_ref):
    inv_qk = 1.0 / MLA_QK
    rope1 = functools.partial(_rope64, rope_ref=rope_ref, c=ROPE_C1, sa=ROPE_SA1, sb=ROPE_SB1)

    def ssq(v):
        return jnp.sum(v * v, axis=-1, keepdims=True)

    cq = _rms(cq_ref[...], qg_ref[...]).astype(BF16)
    q = jnp.dot(cq, wq_ref[...], preferred_element_type=F32)
    gq0, gq1 = gqk_ref[0:1, 0:LANE], gqk_ref[0:1, LANE:2 * LANE]
    gk0, gk1 = gqk_ref[1:2, 0:LANE], gqk_ref[1:2, LANE:2 * LANE]
    for h in range(MLA_HEADS):
        c0 = h * MLA_HEAD_PAD
        b0, b1 = q[:, c0:c0 + LANE], q[:, c0 + LANE:c0 + 2 * LANE]
        r = lax.rsqrt((ssq(b0) + ssq(b1)) * inv_qk + EPS)
        qo_ref[:, c0:c0 + LANE] = (b0 * r * gq0).astype(BF16)
        qo_ref[:, c0 + LANE:c0 + 2 * LANE] = rope1(b1 * r * gq1).astype(BF16)

    ckv = _rms(ckv_ref[...], kvg_ref[...]).astype(BF16)
    kv = jnp.dot(ckv, wkv_ref[...], preferred_element_type=F32)
    nk = MLA_HEADS * MLA_NOPE
    vo_ref[...] = kv[:, nk:].astype(BF16)
    kp = kpe_ref[...]
    skp = ssq(kp)
    for h in range(MLA_HEADS):
        c0 = h * MLA_HEAD_PAD
        kn = kv[:, h * MLA_NOPE:(h + 1) * MLA_NOPE]
        r = lax.rsqrt((ssq(kn) + skp) * inv_qk + EPS)
        ko_ref[:, c0:c0 + LANE] = (kn * r * gk0).astype(BF16)
        ko_ref[:, c0 + LANE:c0 + 2 * LANE] = rope1(kp * r * gk1).astype(BF16)


def _row_spec(tm, width, col_block):
    return pl.BlockSpec((tm, width), lambda i: (i, col_block))


def _full_spec(shape):
    return pl.BlockSpec(shape, lambda i: (0,) * len(shape))


def _rope_spec(tm, lp):
    nblk = lp // tm
    return pl.BlockSpec((tm, ROPE_TABLE_WIDTH), lambda i: (i % nblk, 0))


def _prep_dsa(y, rope, g, lp, tm):
    t = y.shape[0]
    w = BRANCH_WIDTH
    blocks = [4 * _nbytes((tm, w), F32), _nbytes((tm, LANE), F32),
              _nbytes((tm, ROPE_TABLE_WIDTH), F32), 4 * _nbytes((tm, w), BF16),
              _nbytes((tm, 2 * LANE), BF16), _nbytes((tm, LANE), F32)]
    out_w = lambda width: pl.BlockSpec((tm, width), lambda i: (i, 0))
    return pl.pallas_call(
        _prep_dsa_kernel,
        grid=(t // tm,),
        in_specs=[_row_spec(tm, w, COL_DQ // w), _row_spec(tm, w, COL_DK // w),
                  _row_spec(tm, w, COL_DV // w), _row_spec(tm, w, COL_IQ // w),
                  _row_spec(tm, LANE, COL_IKW // LANE), _rope_spec(tm, lp),
                  _full_spec((2, HEAD_DIM))],
        out_specs=[out_w(w), out_w(w), out_w(w), out_w(w), out_w(2 * LANE), out_w(LANE)],
        out_shape=[jax.ShapeDtypeStruct((t, w), BF16)] * 4
        + [jax.ShapeDtypeStruct((t, 2 * LANE), BF16), jax.ShapeDtypeStruct((t, LANE), F32)],
        compiler_params=_params(("parallel",), blocks),
        name="prep_dsa",
    )(y, y, y, y, y, rope, g)


def _prep_diff(y, rope, g, lp, tm):
    t = y.shape[0]
    w = BRANCH_WIDTH
    blocks = [3 * _nbytes((tm, w), F32), _nbytes((tm, ROPE_TABLE_WIDTH), F32),
              3 * _nbytes((tm, w), BF16)]
    out = pl.BlockSpec((tm, w), lambda i: (i, 0))
    return pl.pallas_call(
        _prep_diff_kernel,
        grid=(t // tm,),
        in_specs=[_row_spec(tm, w, COL_FQ // w), _row_spec(tm, w, COL_FK // w),
                  _row_spec(tm, w, COL_FV // w), _rope_spec(tm, lp), _full_spec((2, HEAD_DIM))],
        out_specs=[out, out, out],
        out_shape=[jax.ShapeDtypeStruct((t, w), BF16)] * 3,
        compiler_params=_params(("parallel",), blocks),
        name="prep_diff",
    )(y, y, y, rope, g)


def _prep_mla(y, rope, q_norm_g, kv_norm_g, wq, wkv, gqk, lp, tm):
    t = y.shape[0]
    r = MLA_RANK
    qkw = MLA_HEADS * MLA_HEAD_PAD
    vw = MLA_HEADS * MLA_V
    blocks = [2 * _nbytes((tm, r), F32), _nbytes((tm, LANE), F32),
              _nbytes((tm, ROPE_TABLE_WIDTH), F32), _nbytes((r, qkw), BF16),
              _nbytes((r, 2 * vw), BF16), 2 * _nbytes((tm, qkw), BF16), _nbytes((tm, vw), BF16)]
    scratch = 2 * _nbytes((tm, qkw), F32)
    out_w = lambda width: pl.BlockSpec((tm, width), lambda i: (i, 0))
    return pl.pallas_call(
        _prep_mla_kernel,
        grid=(t // tm,),
        in_specs=[_row_spec(tm, r, COL_CQ // r), _row_spec(tm, r, COL_CKV // r),
                  _row_spec(tm, LANE, COL_KPE // LANE), _rope_spec(tm, lp),
                  _full_spec((1, r)), _full_spec((1, r)), _full_spec((r, qkw)),
                  _full_spec((r, 2 * vw)), _full_spec((2, MLA_HEAD_PAD))],
        out_specs=[out_w(qkw), out_w(qkw), out_w(vw)],
        out_shape=[jax.ShapeDtypeStruct((t, qkw), BF16), jax.ShapeDtypeStruct((t, qkw), BF16),
                   jax.ShapeDtypeStruct((t, vw), BF16)],
        compiler_params=_params(("parallel",), blocks, scratch),
        name="prep_mla",
    )(y, y, y, rope, q_norm_g, kv_norm_g, wq, wkv, gqk)


BIAS_META, BIAS_NONE, BIAS_DIAG, BIAS_META_Q = 0, 1, 2, 3


def _schedule(i, nxb):
    meta_q = i == nxb
    n_steps = jnp.where(meta_q, 1, i + 2)

    def step(c):
        chunk = jnp.where(c == 0, nxb, c - 1)
        kind = jnp.where(meta_q, BIAS_META_Q,
                         jnp.where(c == 0, BIAS_META, jnp.where(c - 1 == i, BIAS_DIAG, BIAS_NONE)))
        return chunk, kind

    return n_steps, step


def _chunk_rows(chunk):
    return pl.ds(pl.multiple_of(chunk * TK, TK), TK)


def _qk(q, k):
    return lax.dot_general(q, k, (((1,), (1,)), ((), ())), preferred_element_type=F32)


def _flash_head(q, k_ref, v_ref, k_cols, v_cols, scale, n_steps, step, bias_fn):
    dv = v_cols.stop - v_cols.start

    def body(c, carry):
        m, l, acc = carry
        chunk, kind = step(c)
        rows = _chunk_rows(chunk)
        s = _qk(q, k_ref[rows, k_cols]) * scale + bias_fn(c, kind)
        m_new = jnp.maximum(m, jnp.max(s, axis=-1, keepdims=True))
        alpha = jnp.exp(m - m_new)
        p = jnp.exp(s - m_new)
        l = alpha * l + jnp.sum(p, axis=-1, keepdims=True)
        acc = alpha * acc + jnp.dot(p.astype(BF16), v_ref[rows, v_cols],
                                    preferred_element_type=F32)
        return m_new, l, acc

    init = (jnp.full((TQ, 1), NEG, F32), jnp.zeros((TQ, 1), F32), jnp.zeros((TQ, dv), F32))
    _, l, acc = lax.fori_loop(0, n_steps, body, init)
    return acc / l


def _mla_attn_kernel(q_ref, k_ref, v_ref, btab_ref, o_ref, *, nxb):
    n_steps, step = _schedule(pl.program_id(1), nxb)
    bias_fn = lambda c, kind: btab_ref[kind]
    for h in range(MLA_HEADS):
        qk_cols = slice(h * MLA_HEAD_PAD, (h + 1) * MLA_HEAD_PAD)
        v_cols = slice(h * MLA_V, (h + 1) * MLA_V)
        o = _flash_head(q_ref[:, qk_cols], k_ref, v_ref, qk_cols, v_cols, MLA_QK ** -0.5,
                        n_steps, step, bias_fn)
        o_ref[:, v_cols] = o.astype(o_ref.dtype)


def _diff_attn_kernel(q_ref, k_ref, v_ref, btab_ref, lam_ref, subg_ref, o_ref, *, nxb,
                      lambda_init):
    n_steps, step = _schedule(pl.program_id(1), nxb)
    bias_fn = lambda c, kind: btab_ref[kind]
    lv = lam_ref[...]
    dot01 = jnp.sum(lv[0:1, :] * lv[1:2, :], axis=-1, keepdims=True)
    dot23 = jnp.sum(lv[2:3, :] * lv[3:4, :], axis=-1, keepdims=True)
    lam = jnp.exp(dot01) - jnp.exp(dot23) + lambda_init
    for h in range(DIFF_HEADS):
        v_cols = slice(h * DIFF_V, (h + 1) * DIFF_V)
        maps = []
        for j in range(2):
            qk_cols = slice((2 * h + j) * HEAD_DIM, (2 * h + j + 1) * HEAD_DIM)
            maps.append(_flash_head(q_ref[:, qk_cols], k_ref, v_ref, qk_cols, v_cols,
                                    HEAD_DIM ** -0.5, n_steps, step, bias_fn))
        o = maps[0] - lam * maps[1]
        o_ref[:, v_cols] = (_rms(o, subg_ref[...]) * (1.0 - lambda_init)).astype(o_ref.dtype)


def _fold(x, op):
    out = x[:, 0:LANE]
    for b in range(1, x.shape[1] // LANE):
        out = op(out, x[:, b * LANE:(b + 1) * LANE])
    return out


def _dsa_attn_kernel(q_ref, iq_ref, iw_ref, k_ref, v_ref, ikz_ref, btab_ref, o_ref, sc_ref, *,
                     nxb, topk):
    n_steps, step = _schedule(pl.program_id(1), nxb)
    kf = float(topk)
    inf = jnp.inf

    def score_step(c, carry):
        chunk, kind = step(c)
        rows = _chunk_rows(chunk)
        acc = jnp.zeros((TQ, TK), F32)
        for p in range(IDX_HEADS // 2):
            iq_pair = iq_ref[:, p * LANE:(p + 1) * LANE]
            for e in range(2):
                logits = _qk(iq_pair, ikz_ref[rows, e * LANE:(e + 1) * LANE])
                w = iw_ref[:, 2 * p + e:2 * p + e + 1]
                acc = acc + jnp.maximum(logits, 0.0) * w
        sc_ref[c] = jnp.where(btab_ref[kind] == 0.0, acc, -inf)
        return carry

    lax.fori_loop(0, n_steps, score_step, 0)

    def count(pred):
        def body(c, part):
            return part + _fold(jnp.where(pred(sc_ref[c]), 1.0, 0.0), jnp.add)

        part = lax.fori_loop(0, n_steps, body, jnp.zeros((TQ, LANE), F32))
        return jnp.sum(part, axis=-1, keepdims=True)

    def min_where(pred):
        def body(c, part):
            s = sc_ref[c]
            return jnp.minimum(part, _fold(jnp.where(pred(s), s, inf), jnp.minimum))

        part = lax.fori_loop(0, n_steps, body, jnp.full((TQ, LANE), inf, F32))
        return jnp.min(part, axis=-1, keepdims=True)

    def stat_body(c, carry):
        mx, mn, nv = carry
        s = sc_ref[c]
        vis = s > -inf
        return (jnp.maximum(mx, _fold(s, jnp.maximum)),
                jnp.minimum(mn, _fold(jnp.where(vis, s, inf), jnp.minimum)),
                nv + _fold(jnp.where(vis, 1.0, 0.0), jnp.add))

    mx, mn, nv = lax.fori_loop(
        0, n_steps, stat_body,
        (jnp.full((TQ, LANE), -inf, F32), jnp.full((TQ, LANE), inf, F32),
         jnp.zeros((TQ, LANE), F32)))
    mx = jnp.max(mx, axis=-1, keepdims=True)
    mn = jnp.min(mn, axis=-1, keepdims=True)
    nv = jnp.sum(nv, axis=-1, keepdims=True)

    def bisect(_, carry):
        lo, hi = carry
        mid = 0.5 * (lo + hi)
        ge = count(lambda s: s >= mid) >= kf
        return jnp.where(ge, mid, lo), jnp.where(ge, hi, mid)

    lo, _ = lax.fori_loop(0, BISECT_STEPS, bisect, (mn, mx))
    t = min_where(lambda s: s >= lo)
    n_above = count(lambda s: s > t)

    def unsettled(state):
        return jnp.max(jnp.where(state[1] >= kf, 1.0, 0.0)) > 0.0

    def raise_threshold(state):
        t, n_above = state
        t = jnp.where(n_above >= kf, min_where(lambda s: s > t), t)
        return t, count(lambda s: s > t)

    t, n_above = lax.while_loop(unsettled, raise_threshold, (t, n_above))
    keep_all = nv <= kf
    t = jnp.where(keep_all, mn, t)
    n_above = jnp.where(keep_all, count(lambda s: s > mn), n_above)

    n_equal_taken = kf - n_above
    upper = (lax.broadcasted_iota(jnp.int32, (TK, TK), 0)
             <= lax.broadcasted_iota(jnp.int32, (TK, TK), 1)).astype(BF16)

    def mask_step(c, seen):
        s = sc_ref[c]
        eq = s == t
        eq_f = jnp.where(eq, 1.0, 0.0)
        prefix = jnp.dot(eq_f.astype(BF16), upper, preferred_element_type=F32)
        sel = (s > t) | (eq & (seen + prefix <= n_equal_taken))
        sc_ref[c] = jnp.where(sel, 0.0, NEG)
        return seen + jnp.sum(eq_f, axis=-1, keepdims=True)

    lax.fori_loop(0, n_steps, mask_step, jnp.zeros((TQ, 1), F32))

    bias_fn = lambda c, kind: sc_ref[c]
    for h in range(DSA_HEADS):
        cols = slice(h * HEAD_DIM, (h + 1) * HEAD_DIM)
        o = _flash_head(q_ref[:, cols], k_ref, v_ref, cols, cols, HEAD_DIM ** -0.5,
                        n_steps, step, bias_fn)
        o_ref[:, cols] = o.astype(o_ref.dtype)


def _attn_specs(nq, lp, widths_q, widths_kv):
    q_specs = [pl.BlockSpec((TQ, w), lambda b, i: (b * nq + i, 0)) for w in widths_q]
    kv_specs = [pl.BlockSpec((lp, w), lambda b, i: (b, 0)) for w in widths_kv]
    btab = pl.BlockSpec((4, TQ, TK), lambda b, i: (0, 0, 0))
    return q_specs, kv_specs, btab


def _attention_call(kernel, name, bsz, nq, lp, q_arrays, kv_arrays, extra, extra_specs,
                    out_width, scratch_shapes=(), scratch_bytes=0):
    q_specs, kv_specs, btab_spec = _attn_specs(
        nq, lp, [a.shape[1] for a in q_arrays], [a.shape[1] for a in kv_arrays])
    blocks = ([_nbytes((TQ, a.shape[1]), a.dtype) for a in q_arrays]
              + [_nbytes((lp, a.shape[1]), a.dtype) for a in kv_arrays]
              + [_nbytes((4, TQ, TK), F32), _nbytes((TQ, out_width), BF16)])
    return pl.pallas_call(
        kernel,
        grid=(bsz, nq),
        in_specs=q_specs + kv_specs + [btab_spec] + extra_specs,
        out_specs=pl.BlockSpec((TQ, out_width), lambda b, i: (b * nq + i, 0)),
        out_shape=jax.ShapeDtypeStruct((bsz * lp, out_width), BF16),
        scratch_shapes=list(scratch_shapes),
        compiler_params=_params(("parallel", "arbitrary"), blocks, scratch_bytes),
        name=name,
    )(*q_arrays, *kv_arrays, *extra)


def _positions(seq):
    lp = seq + TK
    pos = jnp.zeros((lp,), jnp.int32)
    pos = pos.at[:seq].set(N_META + jnp.arange(seq, dtype=jnp.int32))
    return pos.at[seq:seq + N_META].set(jnp.arange(N_META, dtype=jnp.int32))


def _rope_tables(seq):
    pos = _positions(seq).astype(F32)[:, None]

    def cos_sin(d):
        half = d // 2
        inv = ROPE_THETA ** (-jnp.arange(half, dtype=F32) * (2.0 / d))
        ang = pos * inv[None, :]
        return jnp.cos(ang), jnp.sin(ang)

    c, s = cos_sin(HEAD_DIM)
    c2, s2 = cos_sin(IDX_DIM)
    z = jnp.zeros_like(c2)
    cat = lambda *parts: jnp.concatenate(parts, axis=1)
    return cat(cat(c, c), cat(-s, s),
               cat(c2, c2, c2, c2), cat(-s2, z, -s2, z), cat(z, s2, z, s2),
               cat(c2, c2, z, z), cat(-s2, z, z, z), cat(z, s2, z, z))


def _bias_table():
    r = lax.broadcasted_iota(jnp.int32, (TQ, TK), 0)
    c = lax.broadcasted_iota(jnp.int32, (TQ, TK), 1)
    is_meta = c < N_META
    visible = jnp.stack([
        is_meta,
        jnp.ones((TQ, TK), bool),
        c <= r,
        is_meta & ((c <= r) | (r >= N_META)),
    ])
    return jnp.where(visible, 0.0, NEG).astype(F32)


def _pack_w_in(w):
    z = lambda n: jnp.zeros((w.shape[0], n), w.dtype)
    return jnp.concatenate([
        w[:, 0:1024],
        w[:, 1088:4160],
        w[:, 5264:8336],
        w[:, 4160:5184],
        w[:, 8336:14480],
        w[:, 1024:1088], z(64),
        w[:, 5184:5264], z(48),
    ], axis=1).astype(BF16)


def _pack_mla_weights(w_q_up, w_kv_up, qk_g):
    wq = w_q_up.reshape(MLA_RANK, MLA_HEADS, MLA_QK)
    wq = jnp.pad(wq, ((0, 0), (0, 0), (0, MLA_HEAD_PAD - MLA_QK)))
    wq = wq.reshape(MLA_RANK, MLA_HEADS * MLA_HEAD_PAD).astype(BF16)
    wkv = w_kv_up.reshape(MLA_RANK, MLA_HEADS, MLA_NOPE + MLA_V)
    wkv = jnp.concatenate([wkv[:, :, :MLA_NOPE].reshape(MLA_RANK, -1),
                           wkv[:, :, MLA_NOPE:].reshape(MLA_RANK, -1)], axis=1).astype(BF16)
    gqk = jnp.pad(qk_g, ((0, 0), (0, MLA_HEAD_PAD - MLA_QK)))
    return wq, wkv, gqk


def _pack_ffn_in(w, tf):
    d = w.shape[0]
    nf = D_FF // tf
    wab = w.reshape(d, 2, nf, tf).transpose(0, 2, 1, 3)
    return wab.reshape(d, 2 * D_FF).astype(BF16)


def kernel(x, meta, ln1_g, w_in, b_gate, mla_q_norm_g, mla_kv_norm_g, w_mla_q_up, w_mla_kv_up,
           mla_qk_g, dsa_qk_g, diff_qk_g, diff_lambda, diff_subln_g, w_branch, w_o, ln2_g,
           w_ffn_in, w_ffn_out):
    bsz, seq, d = x.shape
    assert d == D_MODEL and seq % TQ == 0
    depth = w_in.shape[0]
    nxb = seq // TQ
    nq = nxb + 1
    lp = seq + TK
    topk = min(TOPK_MAX, seq // 4)
    tm_dense = 512 if (bsz * lp) % 512 == 0 else 256
    tf = 512

    h = jnp.concatenate(
        [x, jnp.broadcast_to(meta[None].astype(x.dtype), (bsz, N_META, d)),
         jnp.zeros((bsz, lp - seq - N_META, d), x.dtype)], axis=1).reshape(bsz * lp, d)
    rope = _rope_tables(seq)
    btab = _bias_table()
    row = lambda v: v.reshape(1, -1)

    for layer in range(depth):
        lambda_init = 0.8 - 0.6 * math.exp(-0.3 * layer)
        y = _norm_matmul(h, row(ln1_g[layer]), _pack_w_in(w_in[layer]), tm_dense, 768)

        wq, wkv, gqk = _pack_mla_weights(w_mla_q_up[layer], w_mla_kv_up[layer], mla_qk_g[layer])
        qm, km, vm = _prep_mla(y, rope, row(mla_q_norm_g[layer]), row(mla_kv_norm_g[layer]),
                               wq, wkv, gqk, lp, TQ)
        qd, kd, vd, iq, ikz, iw = _prep_dsa(y, rope, dsa_qk_g[layer], lp, TQ)
        qf, kf, vf = _prep_diff(y, rope, diff_qk_g[layer], lp, TQ)

        o_a = _attention_call(
            functools.partial(_mla_attn_kernel, nxb=nxb), "attn_mla", bsz, nq, lp,
            [qm], [km, vm], [btab], [], BRANCH_WIDTH)
        sc_shape = (nq, TQ, TK)
        o_b = _attention_call(
            functools.partial(_dsa_attn_kernel, nxb=nxb, topk=topk), "attn_dsa", bsz, nq, lp,
            [qd, iq, iw], [kd, vd, ikz], [btab], [], BRANCH_WIDTH,
            scratch_shapes=[pltpu.VMEM(sc_shape, F32)], scratch_bytes=_nbytes(sc_shape, F32))
        o_c = _attention_call(
            functools.partial(_diff_attn_kernel, nxb=nxb, lambda_init=lambda_init), "attn_diff",
            bsz, nq, lp, [qf], [kf, vf], [btab, diff_lambda[layer], row(diff_subln_g[layer])],
            [pl.BlockSpec((4, HEAD_DIM), lambda b, i: (0, 0)),
             pl.BlockSpec((1, DIFF_V), lambda b, i: (0, 0))], BRANCH_WIDTH)

        merged = _merge(o_a, o_b, o_c, y, row(b_gate[layer]), w_branch[layer].astype(BF16),
                        tm_dense)
        h = _residual_matmul(h, merged, w_o[layer].astype(BF16), tm_dense)
        h = _ffn(h, row(ln2_g[layer]), _pack_ffn_in(w_ffn_in[layer], tf),
                 w_ffn_out[layer].astype(BF16), tm_dense, tf)

    return h.reshape(bsz, lp, d)[:, :seq]
```

```python
import functools
import math

import jax
import jax.numpy as jnp
from jax import lax
from jax.experimental import pallas as pl
from jax.experimental.pallas import tpu as pltpu

F32 = jnp.float32
BF16 = jnp.bfloat16

D_MODEL = 2048
N_META = 16
ROPE_THETA = 10000.0
EPS = 1e-6
TOPK_MAX = 256
MLA_HEADS = 8
MLA_RANK = 512
MLA_NOPE = 128
MLA_ROPE = 64
MLA_V = 128
MLA_QK = MLA_NOPE + MLA_ROPE
DSA_HEADS = 8
IDX_HEADS = 16
IDX_DIM = 64
DIFF_HEADS = 4
DIFF_V = 256
HEAD_DIM = 128
N_BRANCH = 3
BRANCH_WIDTH = 1024
D_FF = 5632

LANE = 128
V7X_VMEM_BYTES = 64 * 1024 * 1024
VMEM_COMPILER_RESERVE = 6 * 1024 * 1024

TQ = 256
TK = 256
MLA_HEAD_PAD = 256
NEG = -1e30
BISECT_STEPS = 22

COL_CQ, COL_CKV, COL_DQ, COL_DK, COL_DV = 0, 512, 1024, 2048, 3072
COL_FQ, COL_FK, COL_FV, COL_IQ, COL_G = 4096, 5120, 6144, 7168, 8192
COL_KPE, COL_IKW, D_IN_PACKED = 14336, 14464, 14592


def _vmem_limit(block_bytes, scratch_bytes=0):
    need = 2 * sum(block_bytes) + scratch_bytes + VMEM_COMPILER_RESERVE
    return int(min(need, V7X_VMEM_BYTES - 2 * 1024 * 1024))


def _nbytes(shape, dtype):
    return math.prod(shape) * jnp.dtype(dtype).itemsize


def _params(sem, block_bytes, scratch_bytes=0):
    return pltpu.CompilerParams(dimension_semantics=sem,
                                vmem_limit_bytes=_vmem_limit(block_bytes, scratch_bytes))


def _rms(x, g):
    ms = jnp.mean(x * x, axis=-1, keepdims=True)
    return x * lax.rsqrt(ms + EPS) * g


def _sigmoid(x):
    return 1.0 / (1.0 + jnp.exp(-x))


def _norm_matmul_kernel(x_ref, g_ref, w_ref, o_ref, xn_ref):
    @pl.when(pl.program_id(1) == 0)
    def _():
        xn_ref[...] = _rms(x_ref[...], g_ref[...]).astype(BF16)

    o_ref[...] = jnp.dot(xn_ref[...], w_ref[...], preferred_element_type=F32)


def _norm_matmul(x, g, w, tm, tn):
    t, d = x.shape
    n = w.shape[1]
    blocks = [_nbytes((tm, d), F32), _nbytes((d, tn), BF16), _nbytes((tm, tn), F32)]
    return pl.pallas_call(
        _norm_matmul_kernel,
        grid=(t // tm, n // tn),
        in_specs=[pl.BlockSpec((tm, d), lambda i, j: (i, 0)),
                  pl.BlockSpec((1, d), lambda i, j: (0, 0)),
                  pl.BlockSpec((d, tn), lambda i, j: (0, j))],
        out_specs=pl.BlockSpec((tm, tn), lambda i, j: (i, j)),
        out_shape=jax.ShapeDtypeStruct((t, n), F32),
        scratch_shapes=[pltpu.VMEM((tm, d), BF16)],
        compiler_params=_params(("parallel", "arbitrary"), blocks, _nbytes((tm, d), BF16)),
        name="inproj",
    )(x, g, w)


def _ffn_kernel(h_ref, g_ref, wa_ref, wb_ref, w2_ref, o_ref, xn_ref, acc_ref):
    f = pl.program_id(1)

    @pl.when(f == 0)
    def _():
        xn_ref[...] = _rms(h_ref[...], g_ref[...]).astype(BF16)
        acc_ref[...] = jnp.zeros_like(acc_ref)

    xn = xn_ref[...]
    a = jnp.dot(xn, wa_ref[...], preferred_element_type=F32)
    act = (a * _sigmoid(a)) * jnp.dot(xn, wb_ref[...], preferred_element_type=F32)
    acc_ref[...] += jnp.dot(act.astype(BF16), w2_ref[...], preferred_element_type=F32)

    @pl.when(f == pl.num_programs(1) - 1)
    def _():
        o_ref[...] = h_ref[...] + acc_ref[...]


def _ffn(h, g, w1, w2, tm, tf):
    t, d = h.shape
    nf = w2.shape[0] // tf
    blocks = [_nbytes((tm, d), F32), 2 * _nbytes((d, tf), BF16), _nbytes((tf, d), BF16),
              _nbytes((tm, d), F32)]
    scratch = _nbytes((tm, d), BF16) + _nbytes((tm, d), F32)
    return pl.pallas_call(
        _ffn_kernel,
        grid=(t // tm, nf),
        in_specs=[pl.BlockSpec((tm, d), lambda i, f: (i, 0)),
                  pl.BlockSpec((1, d), lambda i, f: (0, 0)),
                  pl.BlockSpec((d, tf), lambda i, f: (0, f)),
                  pl.BlockSpec((d, tf), lambda i, f: (0, nf + f)),
                  pl.BlockSpec((tf, d), lambda i, f: (f, 0))],
        out_specs=pl.BlockSpec((tm, d), lambda i, f: (i, 0)),
        out_shape=jax.ShapeDtypeStruct((t, d), F32),
        scratch_shapes=[pltpu.VMEM((tm, d), BF16), pltpu.VMEM((tm, d), F32)],
        compiler_params=_params(("parallel", "arbitrary"), blocks, scratch),
        name="ffn",
    )(h, g, w1, w1, w2)


def _merge_kernel(oa_ref, ob_ref, oc_ref, g_ref, bg_ref, w_ref, o_ref, acc_ref):
    k = pl.program_id(1)
    gate = _sigmoid(g_ref[...] + bg_ref[...])

    def contrib(o_branch_ref):
        return gate * jnp.dot(o_branch_ref[...], w_ref[0], preferred_element_type=F32)

    @pl.when(k == 0)
    def _():
        acc_ref[...] = contrib(oa_ref)

    @pl.when(k == 1)
    def _():
        acc_ref[...] += contrib(ob_ref)

    @pl.when(k == 2)
    def _():
        o_ref[...] = (acc_ref[...] + contrib(oc_ref)).astype(o_ref.dtype)


def _merge(o_a, o_b, o_c, y, b_gate, w_branch, tm):
    t = o_a.shape[0]
    d = D_MODEL
    gate_block0 = COL_G // d
    blocks = [3 * _nbytes((tm, BRANCH_WIDTH), BF16), _nbytes((tm, d), F32),
              _nbytes((BRANCH_WIDTH, d), BF16), _nbytes((tm, d), BF16)]
    o_spec = pl.BlockSpec((tm, BRANCH_WIDTH), lambda i, k: (i, 0))
    return pl.pallas_call(
        _merge_kernel,
        grid=(t // tm, N_BRANCH),
        in_specs=[o_spec, o_spec, o_spec,
                  pl.BlockSpec((tm, d), lambda i, k: (i, gate_block0 + k)),
                  pl.BlockSpec((1, d), lambda i, k: (0, k)),
                  pl.BlockSpec((1, BRANCH_WIDTH, d), lambda i, k: (k, 0, 0))],
        out_specs=pl.BlockSpec((tm, d), lambda i, k: (i, 0)),
        out_shape=jax.ShapeDtypeStruct((t, d), BF16),
        scratch_shapes=[pltpu.VMEM((tm, d), F32)],
        compiler_params=_params(("parallel", "arbitrary"), blocks, _nbytes((tm, d), F32)),
        name="merge",
    )(o_a, o_b, o_c, y, b_gate, w_branch)


def _residual_matmul_kernel(h_ref, a_ref, w_ref, o_ref):
    o_ref[...] = h_ref[...] + jnp.dot(a_ref[...], w_ref[...], preferred_element_type=F32)


def _residual_matmul(h, a, w, tm):
    t, d = h.shape
    kdim = a.shape[1]
    blocks = [_nbytes((tm, d), F32), _nbytes((tm, kdim), BF16), _nbytes((kdim, d), BF16),
              _nbytes((tm, d), F32)]
    return pl.pallas_call(
        _residual_matmul_kernel,
        grid=(t // tm,),
        in_specs=[pl.BlockSpec((tm, d), lambda i: (i, 0)),
                  pl.BlockSpec((tm, kdim), lambda i: (i, 0)),
                  pl.BlockSpec((kdim, d), lambda i: (0, 0))],
        out_specs=pl.BlockSpec((tm, d), lambda i: (i, 0)),
        out_shape=jax.ShapeDtypeStruct((t, d), F32),
        compiler_params=_params(("parallel",), blocks),
        name="wo",
    )(h, a, w)


ROPE_COS, ROPE_SIN, ROPE_C2, ROPE_SA2, ROPE_SB2, ROPE_C1, ROPE_SA1, ROPE_SB1 = (
    k * LANE for k in range(8))
ROPE_TABLE_WIDTH = 8 * LANE


def _tab(rope_ref, off):
    return rope_ref[:, off:off + LANE]


def _rope128(y, rope_ref):
    return y * _tab(rope_ref, ROPE_COS) + pltpu.roll(y, 64, 1) * _tab(rope_ref, ROPE_SIN)


def _rope64(y, rope_ref, c, sa, sb):
    return (y * _tab(rope_ref, c) + pltpu.roll(y, 96, 1) * _tab(rope_ref, sa)
            + pltpu.roll(y, 32, 1) * _tab(rope_ref, sb))


def _norm_rope_heads(x_ref, g, rope_ref, o_ref):
    for h in range(x_ref.shape[1] // HEAD_DIM):
        sl = slice(h * HEAD_DIM, (h + 1) * HEAD_DIM)
        o_ref[:, sl] = _rope128(_rms(x_ref[:, sl], g), rope_ref).astype(BF16)


def _store_transposed(v, vt_ref):
    for b in range(v.shape[1] // LANE):
        sl = slice(b * LANE, (b + 1) * LANE)
        vt_ref[0, sl, :] = v[:, sl].T.astype(BF16)


def _prep_dsa_kernel(dq_ref, dk_ref, dv_ref, iq_ref, ikw_ref, rope_ref, g_ref,
                     qo_ref, ko_ref, vto_ref, iqo_ref, ikzo_ref, iwto_ref):
    _norm_rope_heads(dq_ref, g_ref[0:1, :], rope_ref, qo_ref)
    _norm_rope_heads(dk_ref, g_ref[1:2, :], rope_ref, ko_ref)
    _store_transposed(dv_ref[...], vto_ref)
    for p in range(IDX_HEADS // 2):
        sl = slice(p * LANE, (p + 1) * LANE)
        iqo_ref[:, sl] = _rope64(iq_ref[:, sl], rope_ref, ROPE_C2, ROPE_SA2, ROPE_SB2).astype(BF16)
    x = ikw_ref[...]
    ik = _rope64(x, rope_ref, ROPE_C1, ROPE_SA1, ROPE_SB1)
    ikzo_ref[:, 0:LANE] = ik.astype(BF16)
    ikzo_ref[:, LANE:2 * LANE] = pltpu.roll(ik, 64, 1).astype(BF16)
    w = pltpu.roll(x, 64, 1) * (IDX_HEADS ** -0.5 * IDX_DIM ** -0.5)
    iwto_ref[...] = w.T[0:IDX_HEADS, :]


def _prep_diff_kernel(fq_ref, fk_ref, fv_ref, rope_ref, g_ref, qo_ref, ko_ref, vto_ref):
    _norm_rope_heads(fq_ref, g_ref[0:1, :], rope_ref, qo_ref)
    _norm_rope_heads(fk_ref, g_ref[1:2, :], rope_ref, ko_ref)
    _store_transposed(fv_ref[...], vto_ref)


def _prep_mla_kernel(cq_ref, ckv_ref, kpe_ref, rope_ref, qg_ref, kvg_ref, wq_ref, wkv_ref,
                     gqk_ref, qo_ref, ko_ref, vto_ref):
    inv_qk = 1.0 / MLA_QK
    rope1 = functools.partial(_rope64, rope_ref=rope_ref, c=ROPE_C1, sa=ROPE_SA1, sb=ROPE_SB1)

    def ssq(v):
        return jnp.sum(v * v, axis=-1, keepdims=True)

    cq = _rms(cq_ref[...], qg_ref[...]).astype(BF16)
    q = jnp.dot(cq, wq_ref[...], preferred_element_type=F32)
    gq0, gq1 = gqk_ref[0:1, 0:LANE], gqk_ref[0:1, LANE:2 * LANE]
    gk0, gk1 = gqk_ref[1:2, 0:LANE], gqk_ref[1:2, LANE:2 * LANE]
    for h in range(MLA_HEADS):
        c0 = h * MLA_HEAD_PAD
        b0, b1 = q[:, c0:c0 + LANE], q[:, c0 + LANE:c0 + 2 * LANE]
        r = lax.rsqrt((ssq(b0) + ssq(b1)) * inv_qk + EPS)
        qo_ref[:, c0:c0 + LANE] = (b0 * r * gq0).astype(BF16)
        qo_ref[:, c0 + LANE:c0 + 2 * LANE] = rope1(b1 * r * gq1).astype(BF16)

    ckv = _rms(ckv_ref[...], kvg_ref[...]).astype(BF16)
    kv = jnp.dot(ckv, wkv_ref[...], preferred_element_type=F32)
    nk = MLA_HEADS * MLA_NOPE
    _store_transposed(kv[:, nk:], vto_ref)
    kp = kpe_ref[...]
    skp = ssq(kp)
    for h in range(MLA_HEADS):
        c0 = h * MLA_HEAD_PAD
        kn = kv[:, h * MLA_NOPE:(h + 1) * MLA_NOPE]
        r = lax.rsqrt((ssq(kn) + skp) * inv_qk + EPS)
        ko_ref[:, c0:c0 + LANE] = (kn * r * gk0).astype(BF16)
        ko_ref[:, c0 + LANE:c0 + 2 * LANE] = rope1(kp * r * gk1).astype(BF16)


def _row_spec(tm, width, col_block):
    return pl.BlockSpec((tm, width), lambda i: (i, col_block))


def _full_spec(shape):
    return pl.BlockSpec(shape, lambda i: (0,) * len(shape))


def _rope_spec(tm, lp):
    nblk = lp // tm
    return pl.BlockSpec((tm, ROPE_TABLE_WIDTH), lambda i: (i % nblk, 0))


def _vt_out(t, width):
    return (pl.BlockSpec((1, width, TK), lambda i: (i, 0, 0)),
            jax.ShapeDtypeStruct((t // TK, width, TK), BF16))


def _prep_dsa(y, rope, g, lp):
    t = y.shape[0]
    tm = TK
    w = BRANCH_WIDTH
    blocks = [4 * _nbytes((tm, w), F32), _nbytes((tm, LANE), F32),
              _nbytes((tm, ROPE_TABLE_WIDTH), F32), 4 * _nbytes((tm, w), BF16),
              _nbytes((tm, 2 * LANE), BF16), _nbytes((IDX_HEADS, tm), F32)]
    out_w = lambda width: pl.BlockSpec((tm, width), lambda i: (i, 0))
    vt_spec, vt_shape = _vt_out(t, w)
    return pl.pallas_call(
        _prep_dsa_kernel,
        grid=(t // tm,),
        in_specs=[_row_spec(tm, w, COL_DQ // w), _row_spec(tm, w, COL_DK // w),
                  _row_spec(tm, w, COL_DV // w), _row_spec(tm, w, COL_IQ // w),
                  _row_spec(tm, LANE, COL_IKW // LANE), _rope_spec(tm, lp),
                  _full_spec((2, HEAD_DIM))],
        out_specs=[out_w(w), out_w(w), vt_spec, out_w(w), out_w(2 * LANE),
                   pl.BlockSpec((IDX_HEADS, tm), lambda i: (0, i))],
        out_shape=[jax.ShapeDtypeStruct((t, w), BF16), jax.ShapeDtypeStruct((t, w), BF16),
                   vt_shape, jax.ShapeDtypeStruct((t, w), BF16),
                   jax.ShapeDtypeStruct((t, 2 * LANE), BF16),
                   jax.ShapeDtypeStruct((IDX_HEADS, t), F32)],
        compiler_params=_params(("parallel",), blocks),
        name="prep_dsa",
    )(y, y, y, y, y, rope, g)


def _prep_diff(y, rope, g, lp):
    t = y.shape[0]
    tm = TK
    w = BRANCH_WIDTH
    blocks = [3 * _nbytes((tm, w), F32), _nbytes((tm, ROPE_TABLE_WIDTH), F32),
              3 * _nbytes((tm, w), BF16)]
    out = pl.BlockSpec((tm, w), lambda i: (i, 0))
    vt_spec, vt_shape = _vt_out(t, w)
    return pl.pallas_call(
        _prep_diff_kernel,
        grid=(t // tm,),
        in_specs=[_row_spec(tm, w, COL_FQ // w), _row_spec(tm, w, COL_FK // w),
                  _row_spec(tm, w, COL_FV // w), _rope_spec(tm, lp), _full_spec((2, HEAD_DIM))],
        out_specs=[out, out, vt_spec],
        out_shape=[jax.ShapeDtypeStruct((t, w), BF16), jax.ShapeDtypeStruct((t, w), BF16),
                   vt_shape],
        compiler_params=_params(("parallel",), blocks),
        name="prep_diff",
    )(y, y, y, rope, g)


def _prep_mla(y, rope, q_norm_g, kv_norm_g, wq, wkv, gqk, lp):
    t = y.shape[0]
    tm = TK
    r = MLA_RANK
    qkw = MLA_HEADS * MLA_HEAD_PAD
    vw = MLA_HEADS * MLA_V
    blocks = [2 * _nbytes((tm, r), F32), _nbytes((tm, LANE), F32),
              _nbytes((tm, ROPE_TABLE_WIDTH), F32), _nbytes((r, qkw), BF16),
              _nbytes((r, 2 * vw), BF16), 2 * _nbytes((tm, qkw), BF16), _nbytes((tm, vw), BF16)]
    scratch = 2 * _nbytes((tm, qkw), F32)
    out_w = lambda width: pl.BlockSpec((tm, width), lambda i: (i, 0))
    vt_spec, vt_shape = _vt_out(t, vw)
    return pl.pallas_call(
        _prep_mla_kernel,
        grid=(t // tm,),
        in_specs=[_row_spec(tm, r, COL_CQ // r), _row_spec(tm, r, COL_CKV // r),
                  _row_spec(tm, LANE, COL_KPE // LANE), _rope_spec(tm, lp),
                  _full_spec((1, r)), _full_spec((1, r)), _full_spec((r, qkw)),
                  _full_spec((r, 2 * vw)), _full_spec((2, MLA_HEAD_PAD))],
        out_specs=[out_w(qkw), out_w(qkw), vt_spec],
        out_shape=[jax.ShapeDtypeStruct((t, qkw), BF16), jax.ShapeDtypeStruct((t, qkw), BF16),
                   vt_shape],
        compiler_params=_params(("parallel",), blocks, scratch),
        name="prep_mla",
    )(y, y, y, rope, q_norm_g, kv_norm_g, wq, wkv, gqk)


BIAS_META, BIAS_NONE, BIAS_DIAG, BIAS_META_Q = 0, 1, 2, 3
LOG2E = math.log2(math.e)


def _schedule(i, nxb):
    meta_q = i == nxb
    n_steps = jnp.where(meta_q, 1, i + 2)

    def step(c):
        chunk = jnp.where(c == 0, nxb, c - 1)
        kind = jnp.where(meta_q, BIAS_META_Q,
                         jnp.where(c == 0, BIAS_META, jnp.where(c - 1 == i, BIAS_DIAG, BIAS_NONE)))
        return chunk, kind

    return n_steps, step


def _chunk_rows(chunk):
    return pl.ds(pl.multiple_of(chunk * TK, TK), TK)


def _kq(k, q):
    return lax.dot_general(k, q, (((1,), (1,)), ((), ())), preferred_element_type=F32)


def _fold_rows(x, op):
    parts = [x[r:r + 8, :] for r in range(0, x.shape[0], 8)]
    while len(parts) > 1:
        parts = [op(parts[j], parts[j + 1]) for j in range(0, len(parts) - 1, 2)] + (
            [parts[-1]] if len(parts) % 2 else [])
    return parts[0]


def _flash_heads(q_ref, k_ref, vt_ref, heads, scale, n_steps, step, bias_fn, acc_ref, s_ref):
    nh = len(heads)
    scale2 = scale * LOG2E
    acc_ref[...] = jnp.zeros_like(acc_ref)

    def scores(c):
        chunk, kind = step(c)
        rows = _chunk_rows(chunk)
        bias = bias_fn(c, kind)
        for h, (qk_cols, _) in enumerate(heads):
            s_ref[h] = _kq(k_ref[rows, qk_cols], q_ref[:, qk_cols]) * scale2 + bias

    scores(0)

    def body(c, carry):
        ms, ls = carry
        chunk, _ = step(c)
        new_ms, new_ls = [], []
        for h, (_, v_rows) in enumerate(heads):
            s = s_ref[h]
            m_new = jnp.maximum(ms[h], jnp.max(s, axis=0, keepdims=True))
            alpha = jnp.exp2(ms[h] - m_new)
            p = jnp.exp2(s - m_new)
            new_ms.append(m_new)
            new_ls.append(alpha * ls[h] + jnp.sum(p, axis=0, keepdims=True))
            pv = jnp.dot(vt_ref[chunk, v_rows, :], p.astype(BF16), preferred_element_type=F32)
            acc_ref[h] = alpha * acc_ref[h] + pv
        scores(jnp.minimum(c + 1, n_steps - 1))
        return tuple(new_ms), tuple(new_ls)

    init = (tuple(jnp.full((1, TQ), NEG, F32) for _ in range(nh)),
            tuple(jnp.zeros((1, TQ), F32) for _ in range(nh)))
    _, ls = lax.fori_loop(0, n_steps, body, init)
    return [(acc_ref[h] / ls[h]).T for h in range(nh)]


def _mla_attn_kernel(q_ref, k_ref, vt_ref, btab_ref, o_ref, acc_ref, s_ref, *, nxb):
    n_steps, step = _schedule(pl.program_id(1), nxb)
    heads = [(slice(h * MLA_HEAD_PAD, (h + 1) * MLA_HEAD_PAD), slice(h * MLA_V, (h + 1) * MLA_V))
             for h in range(MLA_HEADS)]
    outs = _flash_heads(q_ref, k_ref, vt_ref, heads, MLA_QK ** -0.5, n_steps, step,
                        lambda c, kind: btab_ref[kind], acc_ref, s_ref)
    for h, o in enumerate(outs):
        o_ref[:, h * MLA_V:(h + 1) * MLA_V] = o.astype(o_ref.dtype)


def _diff_attn_kernel(q_ref, k_ref, vt_ref, btab_ref, lam_ref, subg_ref, o_ref, acc_ref, s_ref,
                      *, nxb, lambda_init):
    n_steps, step = _schedule(pl.program_id(1), nxb)
    lv = lam_ref[...]
    dot01 = jnp.sum(lv[0:1, :] * lv[1:2, :], axis=-1, keepdims=True)
    dot23 = jnp.sum(lv[2:3, :] * lv[3:4, :], axis=-1, keepdims=True)
    lam = jnp.exp(dot01) - jnp.exp(dot23) + lambda_init
    heads = [(slice(j * HEAD_DIM, (j + 1) * HEAD_DIM),
              slice((j // 2) * DIFF_V, (j // 2 + 1) * DIFF_V)) for j in range(2 * DIFF_HEADS)]
    maps = _flash_heads(q_ref, k_ref, vt_ref, heads, HEAD_DIM ** -0.5, n_steps, step,
                        lambda c, kind: btab_ref[kind], acc_ref, s_ref)
    for h in range(DIFF_HEADS):
        o = maps[2 * h] - lam * maps[2 * h + 1]
        o_ref[:, h * DIFF_V:(h + 1) * DIFF_V] = (
            _rms(o, subg_ref[...]) * (1.0 - lambda_init)).astype(o_ref.dtype)


def _dsa_attn_kernel(q_ref, iq_ref, iwt_ref, k_ref, ikz_ref, vt_ref, btab_ref, o_ref,
                     acc_ref, s_ref, sc_ref, *, nxb, topk):
    n_steps, step = _schedule(pl.program_id(1), nxb)
    kf = float(topk)
    inf = jnp.inf

    def score_step(c, carry):
        chunk, kind = step(c)
        rows = _chunk_rows(chunk)
        acc = jnp.zeros((TK, TQ), F32)
        for p in range(IDX_HEADS // 2):
            iq_pair = iq_ref[:, p * LANE:(p + 1) * LANE]
            for e in range(2):
                logits = _kq(ikz_ref[rows, e * LANE:(e + 1) * LANE], iq_pair)
                acc = acc + jnp.maximum(logits, 0.0) * iwt_ref[2 * p + e:2 * p + e + 1, :]
        sc_ref[c] = jnp.where(btab_ref[kind] == 0.0, acc, -inf)
        return carry

    lax.fori_loop(0, n_steps, score_step, 0)

    def count(pred):
        def body(c, part):
            return part + _fold_rows(jnp.where(pred(sc_ref[c]), 1.0, 0.0), jnp.add)

        part = lax.fori_loop(0, n_steps, body, jnp.zeros((8, TQ), F32))
        return jnp.sum(part, axis=0, keepdims=True)

    def min_where(pred):
        def body(c, part):
            s = sc_ref[c]
            return jnp.minimum(part, _fold_rows(jnp.where(pred(s), s, inf), jnp.minimum))

        part = lax.fori_loop(0, n_steps, body, jnp.full((8, TQ), inf, F32))
        return jnp.min(part, axis=0, keepdims=True)

    def stat_body(c, carry):
        mx, mn, nv = carry
        s = sc_ref[c]
        vis = s > -inf
        return (jnp.maximum(mx, _fold_rows(s, jnp.maximum)),
                jnp.minimum(mn, _fold_rows(jnp.where(vis, s, inf), jnp.minimum)),
                nv + _fold_rows(jnp.where(vis, 1.0, 0.0), jnp.add))

    mx, mn, nv = lax.fori_loop(
        0, n_steps, stat_body,
        (jnp.full((8, TQ), -inf, F32), jnp.full((8, TQ), inf, F32), jnp.zeros((8, TQ), F32)))
    mx = jnp.max(mx, axis=0, keepdims=True)
    mn = jnp.min(mn, axis=0, keepdims=True)
    nv = jnp.sum(nv, axis=0, keepdims=True)

    def bisect(_, carry):
        lo, hi = carry
        mid = 0.5 * (lo + hi)
        ge = count(lambda s: s >= mid) >= kf
        return jnp.where(ge, mid, lo), jnp.where(ge, hi, mid)

    lo, _ = lax.fori_loop(0, BISECT_STEPS, bisect, (mn, mx))
    t = min_where(lambda s: s >= lo)
    n_above = count(lambda s: s > t)

    def unsettled(state):
        return jnp.max(jnp.where(state[1] >= kf, 1.0, 0.0)) > 0.0

    def raise_threshold(state):
        t, n_above = state
        t = jnp.where(n_above >= kf, min_where(lambda s: s > t), t)
        return t, count(lambda s: s > t)

    t, n_above = lax.while_loop(unsettled, raise_threshold, (t, n_above))
    keep_all = nv <= kf
    t = jnp.where(keep_all, mn, t)
    n_above = jnp.where(keep_all, count(lambda s: s > mn), n_above)

    n_equal_taken = kf - n_above
    lower = (lax.broadcasted_iota(jnp.int32, (TK, TK), 1)
             <= lax.broadcasted_iota(jnp.int32, (TK, TK), 0)).astype(BF16)

    def mask_step(c, seen):
        s = sc_ref[c]
        eq = s == t
        eq_f = jnp.where(eq, 1.0, 0.0)
        prefix = jnp.dot(lower, eq_f.astype(BF16), preferred_element_type=F32)
        sel = (s > t) | (eq & (seen + prefix <= n_equal_taken))
        sc_ref[c] = jnp.where(sel, 0.0, NEG)
        return seen + jnp.sum(eq_f, axis=0, keepdims=True)

    lax.fori_loop(0, n_steps, mask_step, jnp.zeros((1, TQ), F32))

    heads = [(slice(h * HEAD_DIM, (h + 1) * HEAD_DIM),) * 2 for h in range(DSA_HEADS)]
    outs = _flash_heads(q_ref, k_ref, vt_ref, heads, HEAD_DIM ** -0.5, n_steps, step,
                        lambda c, kind: sc_ref[c], acc_ref, s_ref)
    for h, o in enumerate(outs):
        o_ref[:, h * HEAD_DIM:(h + 1) * HEAD_DIM] = o.astype(o_ref.dtype)


def _attention_call(kernel, name, bsz, nq, lp, q_specs_arrays, k_arrays, vt, extra, extra_specs,
                    out_width, acc_shape, scratch_shapes=()):
    q_arrays = [a for a, _ in q_specs_arrays]
    q_specs = [s for _, s in q_specs_arrays]
    k_specs = [pl.BlockSpec((lp, a.shape[1]), lambda b, i: (b, 0)) for a in k_arrays]
    vt_spec = pl.BlockSpec((nq, vt.shape[1], TK), lambda b, i: (b, 0, 0))
    btab_spec = pl.BlockSpec((4, TK, TQ), lambda b, i: (0, 0, 0))
    scratch = [pltpu.VMEM(acc_shape, F32), pltpu.VMEM((acc_shape[0], TK, TQ), F32)] + list(
        scratch_shapes)
    blocks = ([_nbytes(s.block_shape, a.dtype) for a, s in q_specs_arrays]
              + [_nbytes((lp, a.shape[1]), a.dtype) for a in k_arrays]
              + [_nbytes((nq, vt.shape[1], TK), BF16), _nbytes((4, TK, TQ), F32),
                 _nbytes((TQ, out_width), BF16)])
    scratch_bytes = sum(_nbytes(s.shape, s.dtype) for s in scratch)
    return pl.pallas_call(
        kernel,
        grid=(bsz, nq),
        in_specs=q_specs + k_specs + [vt_spec, btab_spec] + extra_specs,
        out_specs=pl.BlockSpec((TQ, out_width), lambda b, i: (b * nq + i, 0)),
        out_shape=jax.ShapeDtypeStruct((bsz * lp, out_width), BF16),
        scratch_shapes=scratch,
        compiler_params=_params(("parallel", "arbitrary"), blocks, scratch_bytes),
        name=name,
    )(*q_arrays, *k_arrays, vt, *extra)


def _positions(seq):
    lp = seq + TK
    pos = jnp.zeros((lp,), jnp.int32)
    pos = pos.at[:seq].set(N_META + jnp.arange(seq, dtype=jnp.int32))
    return pos.at[seq:seq + N_META].set(jnp.arange(N_META, dtype=jnp.int32))


def _rope_tables(seq):
    pos = _positions(seq).astype(F32)[:, None]

    def cos_sin(d):
        half = d // 2
        inv = ROPE_THETA ** (-jnp.arange(half, dtype=F32) * (2.0 / d))
        ang = pos * inv[None, :]
        return jnp.cos(ang), jnp.sin(ang)

    c, s = cos_sin(HEAD_DIM)
    c2, s2 = cos_sin(IDX_DIM)
    z = jnp.zeros_like(c2)
    cat = lambda *parts: jnp.concatenate(parts, axis=1)
    return cat(cat(c, c), cat(-s, s),
               cat(c2, c2, c2, c2), cat(-s2, z, -s2, z), cat(z, s2, z, s2),
               cat(c2, c2, z, z), cat(-s2, z, z, z), cat(z, s2, z, z))


def _bias_table():
    k = lax.broadcasted_iota(jnp.int32, (TK, TQ), 0)
    q = lax.broadcasted_iota(jnp.int32, (TK, TQ), 1)
    is_meta = k < N_META
    visible = jnp.stack([
        is_meta,
        jnp.ones((TK, TQ), bool),
        k <= q,
        is_meta & ((k <= q) | (q >= N_META)),
    ])
    return jnp.where(visible, 0.0, NEG).astype(F32)


def _pack_w_in(w):
    z = lambda n: jnp.zeros((w.shape[0], n), w.dtype)
    return jnp.concatenate([
        w[:, 0:1024],
        w[:, 1088:4160],
        w[:, 5264:8336],
        w[:, 4160:5184],
        w[:, 8336:14480],
        w[:, 1024:1088], z(64),
        w[:, 5184:5264], z(48),
    ], axis=1).astype(BF16)


def _pack_mla_weights(w_q_up, w_kv_up, qk_g):
    wq = w_q_up.reshape(MLA_RANK, MLA_HEADS, MLA_QK)
    wq = jnp.pad(wq, ((0, 0), (0, 0), (0, MLA_HEAD_PAD - MLA_QK)))
    wq = wq.reshape(MLA_RANK, MLA_HEADS * MLA_HEAD_PAD).astype(BF16)
    wkv = w_kv_up.reshape(MLA_RANK, MLA_HEADS, MLA_NOPE + MLA_V)
    wkv = jnp.concatenate([wkv[:, :, :MLA_NOPE].reshape(MLA_RANK, -1),
                           wkv[:, :, MLA_NOPE:].reshape(MLA_RANK, -1)], axis=1).astype(BF16)
    gqk = jnp.pad(qk_g, ((0, 0), (0, MLA_HEAD_PAD - MLA_QK)))
    return wq, wkv, gqk


def kernel(x, meta, ln1_g, w_in, b_gate, mla_q_norm_g, mla_kv_norm_g, w_mla_q_up, w_mla_kv_up,
           mla_qk_g, dsa_qk_g, diff_qk_g, diff_lambda, diff_subln_g, w_branch, w_o, ln2_g,
           w_ffn_in, w_ffn_out):
    bsz, seq, d = x.shape
    assert d == D_MODEL and seq % TQ == 0
    depth = w_in.shape[0]
    nxb = seq // TQ
    nq = nxb + 1
    lp = seq + TK
    topk = min(TOPK_MAX, seq // 4)
    tm_dense = 512 if (bsz * lp) % 512 == 0 else 256
    tm_in = 1024 if (bsz * lp) % 1024 == 0 else tm_dense
    tf = 512

    h = jnp.concatenate(
        [x, jnp.broadcast_to(meta[None].astype(x.dtype), (bsz, N_META, d)),
         jnp.zeros((bsz, lp - seq - N_META, d), x.dtype)], axis=1).reshape(bsz * lp, d)
    rope = _rope_tables(seq)
    btab = _bias_table()
    row = lambda v: v.reshape(1, -1)
    q_block = lambda a: (a, pl.BlockSpec((TQ, a.shape[1]), lambda b, i: (b * nq + i, 0)))

    for layer in range(depth):
        lambda_init = 0.8 - 0.6 * math.exp(-0.3 * layer)
        y = _norm_matmul(h, row(ln1_g[layer]), _pack_w_in(w_in[layer]), tm_in, 768)

        wq, wkv, gqk = _pack_mla_weights(w_mla_q_up[layer], w_mla_kv_up[layer], mla_qk_g[layer])
        qm, km, vtm = _prep_mla(y, rope, row(mla_q_norm_g[layer]), row(mla_kv_norm_g[layer]),
                                wq, wkv, gqk, lp)
        qd, kd, vtd, iq, ikz, iwt = _prep_dsa(y, rope, dsa_qk_g[layer], lp)
        qf, kf, vtf = _prep_diff(y, rope, diff_qk_g[layer], lp)

        o_a = _attention_call(
            functools.partial(_mla_attn_kernel, nxb=nxb), "attn_mla", bsz, nq, lp,
            [q_block(qm)], [km], vtm, [btab], [], BRANCH_WIDTH, (MLA_HEADS, MLA_V, TQ))
        iwt_block = (iwt, pl.BlockSpec((IDX_HEADS, TQ), lambda b, i: (0, b * nq + i)))
        o_b = _attention_call(
            functools.partial(_dsa_attn_kernel, nxb=nxb, topk=topk), "attn_dsa", bsz, nq, lp,
            [q_block(qd), q_block(iq), iwt_block], [kd, ikz], vtd, [btab], [], BRANCH_WIDTH,
            (DSA_HEADS, HEAD_DIM, TQ), scratch_shapes=[pltpu.VMEM((nq, TK, TQ), F32)])
        o_c = _attention_call(
            functools.partial(_diff_attn_kernel, nxb=nxb, lambda_init=lambda_init), "attn_diff",
            bsz, nq, lp, [q_block(qf)], [kf], vtf,
            [btab, diff_lambda[layer], row(diff_subln_g[layer])],
            [pl.BlockSpec((4, HEAD_DIM), lambda b, i: (0, 0)),
             pl.BlockSpec((1, DIFF_V), lambda b, i: (0, 0))], BRANCH_WIDTH,
            (2 * DIFF_HEADS, DIFF_V, TQ))

        merged = _merge(o_a, o_b, o_c, y, row(b_gate[layer]), w_branch[layer].astype(BF16),
                        tm_dense)
        h = _residual_matmul(h, merged, w_o[layer].astype(BF16), tm_dense)
        h = _ffn(h, row(ln2_g[layer]), w_ffn_in[layer].astype(BF16),
                 w_ffn_out[layer].astype(BF16), tm_dense, tf)

    return h.reshape(bsz, lp, d)[:, :seq]
```

```python
import functools
import math

import jax
import jax.numpy as jnp
from jax import lax
from jax.experimental import pallas as pl
from jax.experimental.pallas import tpu as pltpu

F32 = jnp.float32
BF16 = jnp.bfloat16

D_MODEL = 2048
N_META = 16
ROPE_THETA = 10000.0
EPS = 1e-6
TOPK_MAX = 256
MLA_HEADS = 8
MLA_RANK = 512
MLA_NOPE = 128
MLA_ROPE = 64
MLA_V = 128
MLA_QK = MLA_NOPE + MLA_ROPE
DSA_HEADS = 8
IDX_HEADS = 16
IDX_DIM = 64
DIFF_HEADS = 4
DIFF_V = 256
HEAD_DIM = 128
N_BRANCH = 3
BRANCH_WIDTH = 1024
D_FF = 5632

LANE = 128
V7X_VMEM_BYTES = 64 * 1024 * 1024
VMEM_COMPILER_RESERVE = 6 * 1024 * 1024

TQ = 256
TK = 256
MLA_HEAD_PAD = 256
NEG = -1e30
BISECT_STEPS = 16

COL_CQ, COL_CKV, COL_DQ, COL_DK, COL_DV = 0, 512, 1024, 2048, 3072
COL_FQ, COL_FK, COL_FV, COL_IQ, COL_G = 4096, 5120, 6144, 7168, 8192
COL_KPE, COL_IKW, D_IN_PACKED = 14336, 14464, 14592


def _vmem_limit(block_bytes, scratch_bytes=0):
    need = 2 * sum(block_bytes) + scratch_bytes + VMEM_COMPILER_RESERVE
    return int(min(need, V7X_VMEM_BYTES - 2 * 1024 * 1024))


def _nbytes(shape, dtype):
    return math.prod(shape) * jnp.dtype(dtype).itemsize


def _params(sem, block_bytes, scratch_bytes=0):
    return pltpu.CompilerParams(dimension_semantics=sem,
                                vmem_limit_bytes=_vmem_limit(block_bytes, scratch_bytes))


def _rms(x, g):
    ms = jnp.mean(x * x, axis=-1, keepdims=True)
    return x * lax.rsqrt(ms + EPS) * g


def _sigmoid(x):
    return 1.0 / (1.0 + jnp.exp(-x))


def _norm_matmul_kernel(x_ref, g_ref, w_ref, o_ref, xn_ref):
    @pl.when(pl.program_id(1) == 0)
    def _():
        xn_ref[...] = _rms(x_ref[...], g_ref[...]).astype(BF16)

    o_ref[...] = jnp.dot(xn_ref[...], w_ref[...], preferred_element_type=F32)


def _norm_matmul(x, g, w, tm, tn):
    t, d = x.shape
    n = w.shape[1]
    blocks = [_nbytes((tm, d), F32), _nbytes((d, tn), BF16), _nbytes((tm, tn), F32)]
    return pl.pallas_call(
        _norm_matmul_kernel,
        grid=(t // tm, n // tn),
        in_specs=[pl.BlockSpec((tm, d), lambda i, j: (i, 0)),
                  pl.BlockSpec((1, d), lambda i, j: (0, 0)),
                  pl.BlockSpec((d, tn), lambda i, j: (0, j))],
        out_specs=pl.BlockSpec((tm, tn), lambda i, j: (i, j)),
        out_shape=jax.ShapeDtypeStruct((t, n), F32),
        scratch_shapes=[pltpu.VMEM((tm, d), BF16)],
        compiler_params=_params(("parallel", "arbitrary"), blocks, _nbytes((tm, d), BF16)),
        name="inproj",
    )(x, g, w)


def _ffn_kernel(h_ref, g_ref, wa_ref, wb_ref, w2_ref, o_ref, xn_ref):
    @pl.when(pl.program_id(1) == 0)
    def _():
        h = h_ref[...]
        xn_ref[...] = _rms(h, g_ref[...]).astype(BF16)
        o_ref[...] = h

    xn = xn_ref[...]
    a = jnp.dot(xn, wa_ref[...], preferred_element_type=F32)
    act = (a * _sigmoid(a)) * jnp.dot(xn, wb_ref[...], preferred_element_type=F32)
    o_ref[...] += jnp.dot(act.astype(BF16), w2_ref[...], preferred_element_type=F32)


def _ffn(h, g, w1, w2, tm, tf):
    t, d = h.shape
    nf = w2.shape[0] // tf
    blocks = [_nbytes((tm, d), F32), 2 * _nbytes((d, tf), BF16), _nbytes((tf, d), BF16),
              _nbytes((tm, d), F32)]
    live = _nbytes((tm, d), BF16) + 2 * _nbytes((tm, tf), F32) + _nbytes((tm, d), F32)
    return pl.pallas_call(
        _ffn_kernel,
        grid=(t // tm, nf),
        in_specs=[pl.BlockSpec((tm, d), lambda i, f: (i, 0)),
                  pl.BlockSpec((1, d), lambda i, f: (0, 0)),
                  pl.BlockSpec((d, tf), lambda i, f: (0, f)),
                  pl.BlockSpec((d, tf), lambda i, f: (0, nf + f)),
                  pl.BlockSpec((tf, d), lambda i, f: (f, 0))],
        out_specs=pl.BlockSpec((tm, d), lambda i, f: (i, 0)),
        out_shape=jax.ShapeDtypeStruct((t, d), F32),
        scratch_shapes=[pltpu.VMEM((tm, d), BF16)],
        compiler_params=_params(("parallel", "arbitrary"), blocks, live),
        name="ffn",
    )(h, g, w1, w1, w2)


def _merge_wo_kernel(h_ref, oa_ref, ob_ref, oc_ref, ga_ref, gb_ref, gc_ref, bg_ref, wbr_ref,
                     wo_ref, o_ref):
    d = h_ref.shape[1]

    def branch(k, o_branch_ref, g_ref):
        gate = _sigmoid(g_ref[...] + bg_ref[:, k * d:(k + 1) * d])
        return gate * jnp.dot(o_branch_ref[...], wbr_ref[k], preferred_element_type=F32)

    merged = branch(0, oa_ref, ga_ref) + branch(1, ob_ref, gb_ref) + branch(2, oc_ref, gc_ref)
    o_ref[...] = h_ref[...] + jnp.dot(merged.astype(BF16), wo_ref[...],
                                      preferred_element_type=F32)


def _merge_wo(h, o_a, o_b, o_c, y, b_gate, w_branch, w_o, tm):
    t, d = h.shape
    gate_block0 = COL_G // d
    resident = pl.Buffered(1)
    blocks = [2 * _nbytes((tm, d), F32), 3 * _nbytes((tm, BRANCH_WIDTH), BF16),
              3 * _nbytes((tm, d), F32)]
    weights = _nbytes((N_BRANCH, BRANCH_WIDTH, d), BF16) + _nbytes((d, d), BF16)
    o_spec = pl.BlockSpec((tm, BRANCH_WIDTH), lambda i: (i, 0))
    g_spec = lambda k: pl.BlockSpec((tm, d), lambda i: (i, gate_block0 + k))
    return pl.pallas_call(
        _merge_wo_kernel,
        grid=(t // tm,),
        in_specs=[pl.BlockSpec((tm, d), lambda i: (i, 0)), o_spec, o_spec, o_spec,
                  g_spec(0), g_spec(1), g_spec(2),
                  pl.BlockSpec((1, N_BRANCH * d), lambda i: (0, 0)),
                  pl.BlockSpec((N_BRANCH, BRANCH_WIDTH, d), lambda i: (0, 0, 0),
                               pipeline_mode=resident),
                  pl.BlockSpec((d, d), lambda i: (0, 0), pipeline_mode=resident)],
        out_specs=pl.BlockSpec((tm, d), lambda i: (i, 0)),
        out_shape=jax.ShapeDtypeStruct((t, d), F32),
        compiler_params=_params(("parallel",), blocks, weights + 4 * _nbytes((tm, d), F32)),
        name="merge_wo",
    )(h, o_a, o_b, o_c, y, y, y, b_gate, w_branch, w_o)


ROPE_COS, ROPE_SIN, ROPE_C2, ROPE_SA2, ROPE_SB2, ROPE_C1, ROPE_SA1, ROPE_SB1 = (
    k * LANE for k in range(8))
ROPE_TABLE_WIDTH = 8 * LANE


def _tab(rope_ref, off):
    return rope_ref[:, off:off + LANE]


def _rope128(y, rope_ref):
    return y * _tab(rope_ref, ROPE_COS) + pltpu.roll(y, 64, 1) * _tab(rope_ref, ROPE_SIN)


def _rope64(y, rope_ref, c, sa, sb):
    return (y * _tab(rope_ref, c) + pltpu.roll(y, 96, 1) * _tab(rope_ref, sa)
            + pltpu.roll(y, 32, 1) * _tab(rope_ref, sb))


def _norm_rope_heads(x_ref, g, rope_ref, o_ref):
    for h in range(x_ref.shape[1] // HEAD_DIM):
        sl = slice(h * HEAD_DIM, (h + 1) * HEAD_DIM)
        o_ref[:, sl] = _rope128(_rms(x_ref[:, sl], g), rope_ref).astype(BF16)


def _store_transposed(v, vt_ref):
    for b in range(v.shape[1] // LANE):
        sl = slice(b * LANE, (b + 1) * LANE)
        vt_ref[0, sl, :] = v[:, sl].T.astype(BF16)


def _prep_dsa_kernel(dq_ref, dk_ref, dv_ref, iq_ref, ikw_ref, rope_ref, g_ref,
                     qo_ref, ko_ref, vto_ref, iqo_ref, ikzo_ref, iwto_ref):
    _norm_rope_heads(dq_ref, g_ref[0:1, :], rope_ref, qo_ref)
    _norm_rope_heads(dk_ref, g_ref[1:2, :], rope_ref, ko_ref)
    _store_transposed(dv_ref[...], vto_ref)
    for p in range(IDX_HEADS // 2):
        sl = slice(p * LANE, (p + 1) * LANE)
        iqo_ref[:, sl] = _rope64(iq_ref[:, sl], rope_ref, ROPE_C2, ROPE_SA2, ROPE_SB2).astype(BF16)
    x = ikw_ref[...]
    ik = _rope64(x, rope_ref, ROPE_C1, ROPE_SA1, ROPE_SB1)
    ikzo_ref[:, 0:LANE] = ik.astype(BF16)
    ikzo_ref[:, LANE:2 * LANE] = pltpu.roll(ik, 64, 1).astype(BF16)
    w = pltpu.roll(x, 64, 1) * (IDX_HEADS ** -0.5 * IDX_DIM ** -0.5)
    iwto_ref[...] = w.T[0:IDX_HEADS, :]


def _prep_diff_kernel(fq_ref, fk_ref, fv_ref, rope_ref, g_ref, qo_ref, ko_ref, vto_ref):
    _norm_rope_heads(fq_ref, g_ref[0:1, :], rope_ref, qo_ref)
    _norm_rope_heads(fk_ref, g_ref[1:2, :], rope_ref, ko_ref)
    _store_transposed(fv_ref[...], vto_ref)


def _prep_mla_kernel(cq_ref, ckv_ref, kpe_ref, rope_ref, qg_ref, kvg_ref, wq_ref, wkv_ref,
                     gqk_ref, qo_ref, ko_ref, vto_ref):
    inv_qk = 1.0 / MLA_QK
    rope1 = functools.partial(_rope64, rope_ref=rope_ref, c=ROPE_C1, sa=ROPE_SA1, sb=ROPE_SB1)

    def ssq(v):
        return jnp.sum(v * v, axis=-1, keepdims=True)

    cq = _rms(cq_ref[...], qg_ref[...]).astype(BF16)
    q = jnp.dot(cq, wq_ref[...], preferred_element_type=F32)
    gq0, gq1 = gqk_ref[0:1, 0:LANE], gqk_ref[0:1, LANE:2 * LANE]
    gk0, gk1 = gqk_ref[1:2, 0:LANE], gqk_ref[1:2, LANE:2 * LANE]
    for h in range(MLA_HEADS):
        c0 = h * MLA_HEAD_PAD
        b0, b1 = q[:, c0:c0 + LANE], q[:, c0 + LANE:c0 + 2 * LANE]
        r = lax.rsqrt((ssq(b0) + ssq(b1)) * inv_qk + EPS)
        qo_ref[:, c0:c0 + LANE] = (b0 * r * gq0).astype(BF16)
        qo_ref[:, c0 + LANE:c0 + 2 * LANE] = rope1(b1 * r * gq1).astype(BF16)

    ckv = _rms(ckv_ref[...], kvg_ref[...]).astype(BF16)
    kv = jnp.dot(ckv, wkv_ref[...], preferred_element_type=F32)
    nk = MLA_HEADS * MLA_NOPE
    _store_transposed(kv[:, nk:], vto_ref)
    kp = kpe_ref[...]
    skp = ssq(kp)
    for h in range(MLA_HEADS):
        c0 = h * MLA_HEAD_PAD
        kn = kv[:, h * MLA_NOPE:(h + 1) * MLA_NOPE]
        r = lax.rsqrt((ssq(kn) + skp) * inv_qk + EPS)
        ko_ref[:, c0:c0 + LANE] = (kn * r * gk0).astype(BF16)
        ko_ref[:, c0 + LANE:c0 + 2 * LANE] = rope1(kp * r * gk1).astype(BF16)


def _row_spec(tm, width, col_block):
    return pl.BlockSpec((tm, width), lambda i: (i, col_block))


def _full_spec(shape):
    return pl.BlockSpec(shape, lambda i: (0,) * len(shape))


def _rope_spec(tm, lp):
    nblk = lp // tm
    return pl.BlockSpec((tm, ROPE_TABLE_WIDTH), lambda i: (i % nblk, 0))


def _vt_out(t, width):
    return (pl.BlockSpec((1, width, TK), lambda i: (i, 0, 0)),
            jax.ShapeDtypeStruct((t // TK, width, TK), BF16))


def _prep_dsa(y, rope, g, lp):
    t = y.shape[0]
    tm = TK
    w = BRANCH_WIDTH
    blocks = [4 * _nbytes((tm, w), F32), _nbytes((tm, LANE), F32),
              _nbytes((tm, ROPE_TABLE_WIDTH), F32), 4 * _nbytes((tm, w), BF16),
              _nbytes((tm, 2 * LANE), BF16), _nbytes((IDX_HEADS, tm), F32)]
    out_w = lambda width: pl.BlockSpec((tm, width), lambda i: (i, 0))
    vt_spec, vt_shape = _vt_out(t, w)
    return pl.pallas_call(
        _prep_dsa_kernel,
        grid=(t // tm,),
        in_specs=[_row_spec(tm, w, COL_DQ // w), _row_spec(tm, w, COL_DK // w),
                  _row_spec(tm, w, COL_DV // w), _row_spec(tm, w, COL_IQ // w),
                  _row_spec(tm, LANE, COL_IKW // LANE), _rope_spec(tm, lp),
                  _full_spec((2, HEAD_DIM))],
        out_specs=[out_w(w), out_w(w), vt_spec, out_w(w), out_w(2 * LANE),
                   pl.BlockSpec((IDX_HEADS, tm), lambda i: (0, i))],
        out_shape=[jax.ShapeDtypeStruct((t, w), BF16), jax.ShapeDtypeStruct((t, w), BF16),
                   vt_shape, jax.ShapeDtypeStruct((t, w), BF16),
                   jax.ShapeDtypeStruct((t, 2 * LANE), BF16),
                   jax.ShapeDtypeStruct((IDX_HEADS, t), F32)],
        compiler_params=_params(("parallel",), blocks),
        name="prep_dsa",
    )(y, y, y, y, y, rope, g)


def _prep_diff(y, rope, g, lp):
    t = y.shape[0]
    tm = TK
    w = BRANCH_WIDTH
    blocks = [3 * _nbytes((tm, w), F32), _nbytes((tm, ROPE_TABLE_WIDTH), F32),
              3 * _nbytes((tm, w), BF16)]
    out = pl.BlockSpec((tm, w), lambda i: (i, 0))
    vt_spec, vt_shape = _vt_out(t, w)
    return pl.pallas_call(
        _prep_diff_kernel,
        grid=(t // tm,),
        in_specs=[_row_spec(tm, w, COL_FQ // w), _row_spec(tm, w, COL_FK // w),
                  _row_spec(tm, w, COL_FV // w), _rope_spec(tm, lp), _full_spec((2, HEAD_DIM))],
        out_specs=[out, out, vt_spec],
        out_shape=[jax.ShapeDtypeStruct((t, w), BF16), jax.ShapeDtypeStruct((t, w), BF16),
                   vt_shape],
        compiler_params=_params(("parallel",), blocks),
        name="prep_diff",
    )(y, y, y, rope, g)


def _prep_mla(y, rope, q_norm_g, kv_norm_g, wq, wkv, gqk, lp):
    t = y.shape[0]
    tm = TK
    r = MLA_RANK
    qkw = MLA_HEADS * MLA_HEAD_PAD
    vw = MLA_HEADS * MLA_V
    blocks = [2 * _nbytes((tm, r), F32), _nbytes((tm, LANE), F32),
              _nbytes((tm, ROPE_TABLE_WIDTH), F32), _nbytes((r, qkw), BF16),
              _nbytes((r, 2 * vw), BF16), 2 * _nbytes((tm, qkw), BF16), _nbytes((tm, vw), BF16)]
    scratch = 2 * _nbytes((tm, qkw), F32)
    out_w = lambda width: pl.BlockSpec((tm, width), lambda i: (i, 0))
    vt_spec, vt_shape = _vt_out(t, vw)
    return pl.pallas_call(
        _prep_mla_kernel,
        grid=(t // tm,),
        in_specs=[_row_spec(tm, r, COL_CQ // r), _row_spec(tm, r, COL_CKV // r),
                  _row_spec(tm, LANE, COL_KPE // LANE), _rope_spec(tm, lp),
                  _full_spec((1, r)), _full_spec((1, r)), _full_spec((r, qkw)),
                  _full_spec((r, 2 * vw)), _full_spec((2, MLA_HEAD_PAD))],
        out_specs=[out_w(qkw), out_w(qkw), vt_spec],
        out_shape=[jax.ShapeDtypeStruct((t, qkw), BF16), jax.ShapeDtypeStruct((t, qkw), BF16),
                   vt_shape],
        compiler_params=_params(("parallel",), blocks, scratch),
        name="prep_mla",
    )(y, y, y, rope, q_norm_g, kv_norm_g, wq, wkv, gqk)


BIAS_META, BIAS_NONE, BIAS_DIAG, BIAS_META_Q = 0, 1, 2, 3
LOG2E = math.log2(math.e)


def _schedule(i, nxb):
    meta_q = i == nxb
    n_steps = jnp.where(meta_q, 1, i + 2)

    def step(c):
        chunk = jnp.where(c == 0, nxb, c - 1)
        kind = jnp.where(meta_q, BIAS_META_Q,
                         jnp.where(c == 0, BIAS_META, jnp.where(c - 1 == i, BIAS_DIAG, BIAS_NONE)))
        return chunk, kind

    return n_steps, step


def _chunk_rows(chunk):
    return pl.ds(pl.multiple_of(chunk * TK, TK), TK)


def _kq(k, q):
    return lax.dot_general(k, q, (((1,), (1,)), ((), ())), preferred_element_type=F32)


def _fold_rows(x, op):
    parts = [x[r:r + 8, :] for r in range(0, x.shape[0], 8)]
    while len(parts) > 1:
        parts = [op(parts[j], parts[j + 1]) for j in range(0, len(parts) - 1, 2)] + (
            [parts[-1]] if len(parts) % 2 else [])
    return parts[0]


def _flash_heads(q_ref, k_ref, vt_ref, heads, scale, n_steps, step, bias_fn, acc_ref, s_ref):
    nh = len(heads)
    scale2 = scale * LOG2E
    acc_ref[...] = jnp.zeros_like(acc_ref)

    def scores(c):
        chunk, kind = step(c)
        rows = _chunk_rows(chunk)
        bias = bias_fn(c, kind)
        for h, (qk_cols, _) in enumerate(heads):
            s_ref[h] = _kq(k_ref[rows, qk_cols], q_ref[:, qk_cols]) * scale2 + bias

    scores(0)

    def body(c, carry):
        ms, ls = carry
        chunk, _ = step(c)
        new_ms, new_ls = [], []
        for h, (_, v_rows) in enumerate(heads):
            s = s_ref[h]
            m_new = jnp.maximum(ms[h], jnp.max(s, axis=0, keepdims=True))
            alpha = jnp.exp2(ms[h] - m_new)
            p = jnp.exp2(s - m_new)
            new_ms.append(m_new)
            new_ls.append(alpha * ls[h] + jnp.sum(p, axis=0, keepdims=True))
            pv = jnp.dot(vt_ref[chunk, v_rows, :], p.astype(BF16), preferred_element_type=F32)
            acc_ref[h] = alpha * acc_ref[h] + pv
        scores(jnp.minimum(c + 1, n_steps - 1))
        return tuple(new_ms), tuple(new_ls)

    init = (tuple(jnp.full((1, TQ), NEG, F32) for _ in range(nh)),
            tuple(jnp.zeros((1, TQ), F32) for _ in range(nh)))
    _, ls = lax.fori_loop(0, n_steps, body, init)
    return [acc_ref[h] / ls[h] for h in range(nh)]


def _mla_attn_kernel(q_ref, k_ref, vt_ref, btab_ref, o_ref, acc_ref, s_ref, *, nxb):
    n_steps, step = _schedule(pl.program_id(1), nxb)
    heads = [(slice(h * MLA_HEAD_PAD, (h + 1) * MLA_HEAD_PAD), slice(h * MLA_V, (h + 1) * MLA_V))
             for h in range(MLA_HEADS)]
    outs = _flash_heads(q_ref, k_ref, vt_ref, heads, MLA_QK ** -0.5, n_steps, step,
                        lambda c, kind: btab_ref[kind], acc_ref, s_ref)
    for h, o in enumerate(outs):
        o_ref[:, h * MLA_V:(h + 1) * MLA_V] = o.T.astype(o_ref.dtype)


def _diff_attn_kernel(q_ref, k_ref, vt_ref, btab_ref, lam_ref, subg_ref, o_ref, acc_ref, s_ref,
                      *, nxb, lambda_init):
    n_steps, step = _schedule(pl.program_id(1), nxb)
    lv = lam_ref[...]
    dot01 = jnp.sum(lv[0:1, :] * lv[1:2, :], axis=-1, keepdims=True)
    dot23 = jnp.sum(lv[2:3, :] * lv[3:4, :], axis=-1, keepdims=True)
    lam = jnp.exp(dot01) - jnp.exp(dot23) + lambda_init
    heads = [(slice(j * HEAD_DIM, (j + 1) * HEAD_DIM),
              slice((j // 2) * DIFF_V, (j // 2 + 1) * DIFF_V)) for j in range(2 * DIFF_HEADS)]
    maps = _flash_heads(q_ref, k_ref, vt_ref, heads, HEAD_DIM ** -0.5, n_steps, step,
                        lambda c, kind: btab_ref[kind], acc_ref, s_ref)
    for h in range(DIFF_HEADS):
        o = (maps[2 * h] - lam * maps[2 * h + 1]).T
        o_ref[:, h * DIFF_V:(h + 1) * DIFF_V] = (
            _rms(o, subg_ref[...]) * (1.0 - lambda_init)).astype(o_ref.dtype)


def _dsa_attn_kernel(q_ref, iq_ref, iwt_ref, k_ref, ikz_ref, vt_ref, btab_ref, o_ref,
                     acc_ref, s_ref, sc_ref, *, nxb, topk):
    n_steps, step = _schedule(pl.program_id(1), nxb)
    kf = float(topk)
    inf = jnp.inf

    def score_step(c, carry):
        chunk, kind = step(c)
        rows = _chunk_rows(chunk)
        acc = jnp.zeros((TK, TQ), F32)
        for p in range(IDX_HEADS // 2):
            iq_pair = iq_ref[:, p * LANE:(p + 1) * LANE]
            for e in range(2):
                logits = _kq(ikz_ref[rows, e * LANE:(e + 1) * LANE], iq_pair)
                acc = acc + jnp.maximum(logits, 0.0) * iwt_ref[2 * p + e:2 * p + e + 1, :]
        sc_ref[c] = jnp.where(btab_ref[kind] == 0.0, acc, -inf)
        return carry

    lax.fori_loop(0, n_steps, score_step, 0)

    def count(pred):
        def body(c, part):
            return part + _fold_rows(jnp.where(pred(sc_ref[c]), 1.0, 0.0), jnp.add)

        part = lax.fori_loop(0, n_steps, body, jnp.zeros((8, TQ), F32))
        return jnp.sum(part, axis=0, keepdims=True)

    def min_where(pred):
        def body(c, part):
            s = sc_ref[c]
            return jnp.minimum(part, _fold_rows(jnp.where(pred(s), s, inf), jnp.minimum))

        part = lax.fori_loop(0, n_steps, body, jnp.full((8, TQ), inf, F32))
        return jnp.min(part, axis=0, keepdims=True)

    def stat_body(c, carry):
        mx, mn, nv = carry
        s = sc_ref[c]
        vis = s > -inf
        return (jnp.maximum(mx, _fold_rows(s, jnp.maximum)),
                jnp.minimum(mn, _fold_rows(jnp.where(vis, s, inf), jnp.minimum)),
                nv + _fold_rows(jnp.where(vis, 1.0, 0.0), jnp.add))

    mx, mn, nv = lax.fori_loop(
        0, n_steps, stat_body,
        (jnp.full((8, TQ), -inf, F32), jnp.full((8, TQ), inf, F32), jnp.zeros((8, TQ), F32)))
    mx = jnp.max(mx, axis=0, keepdims=True)
    mn = jnp.min(mn, axis=0, keepdims=True)
    nv = jnp.sum(nv, axis=0, keepdims=True)

    def bisect(_, carry):
        lo, hi = carry
        mid = 0.5 * (lo + hi)
        ge = count(lambda s: s >= mid) >= kf
        return jnp.where(ge, mid, lo), jnp.where(ge, hi, mid)

    lo, _ = lax.fori_loop(0, BISECT_STEPS, bisect, (mn, mx))
    t = min_where(lambda s: s >= lo)
    n_above = count(lambda s: s > t)

    def unsettled(state):
        return jnp.max(jnp.where(state[1] >= kf, 1.0, 0.0)) > 0.0

    def raise_threshold(state):
        t, n_above = state
        t = jnp.where(n_above >= kf, min_where(lambda s: s > t), t)
        return t, count(lambda s: s > t)

    t, n_above = lax.while_loop(unsettled, raise_threshold, (t, n_above))
    keep_all = nv <= kf
    t = jnp.where(keep_all, mn, t)
    n_above = jnp.where(keep_all, count(lambda s: s > mn), n_above)

    n_equal_taken = kf - n_above
    lower = (lax.broadcasted_iota(jnp.int32, (TK, TK), 1)
             <= lax.broadcasted_iota(jnp.int32, (TK, TK), 0)).astype(BF16)

    def mask_step(c, seen):
        s = sc_ref[c]
        eq = s == t
        eq_f = jnp.where(eq, 1.0, 0.0)
        prefix = jnp.dot(lower, eq_f.astype(BF16), preferred_element_type=F32)
        sel = (s > t) | (eq & (seen + prefix <= n_equal_taken))
        sc_ref[c] = jnp.where(sel, 0.0, NEG)
        return seen + jnp.sum(eq_f, axis=0, keepdims=True)

    lax.fori_loop(0, n_steps, mask_step, jnp.zeros((1, TQ), F32))

    heads = [(slice(h * HEAD_DIM, (h + 1) * HEAD_DIM),) * 2 for h in range(DSA_HEADS)]
    outs = _flash_heads(q_ref, k_ref, vt_ref, heads, HEAD_DIM ** -0.5, n_steps, step,
                        lambda c, kind: sc_ref[c], acc_ref, s_ref)
    for h, o in enumerate(outs):
        o_ref[:, h * HEAD_DIM:(h + 1) * HEAD_DIM] = o.T.astype(o_ref.dtype)


def _attention_call(kernel, name, bsz, nq, lp, q_specs_arrays, k_arrays, vt, extra, extra_specs,
                    out_width, acc_shape, scratch_shapes=()):
    q_arrays = [a for a, _ in q_specs_arrays]
    q_specs = [s for _, s in q_specs_arrays]
    k_specs = [pl.BlockSpec((lp, a.shape[1]), lambda b, i: (b, 0)) for a in k_arrays]
    vt_spec = pl.BlockSpec((nq, vt.shape[1], TK), lambda b, i: (b, 0, 0))
    btab_spec = pl.BlockSpec((4, TK, TQ), lambda b, i: (0, 0, 0))
    scratch = [pltpu.VMEM(acc_shape, F32), pltpu.VMEM((acc_shape[0], TK, TQ), F32)] + list(
        scratch_shapes)
    blocks = ([_nbytes(s.block_shape, a.dtype) for a, s in q_specs_arrays]
              + [_nbytes((lp, a.shape[1]), a.dtype) for a in k_arrays]
              + [_nbytes((nq, vt.shape[1], TK), BF16), _nbytes((4, TK, TQ), F32),
                 _nbytes((TQ, out_width), BF16)])
    scratch_bytes = sum(_nbytes(s.shape, s.dtype) for s in scratch)
    return pl.pallas_call(
        kernel,
        grid=(bsz, nq),
        in_specs=q_specs + k_specs + [vt_spec, btab_spec] + extra_specs,
        out_specs=pl.BlockSpec((TQ, out_width), lambda b, i: (b * nq + i, 0)),
        out_shape=jax.ShapeDtypeStruct((bsz * lp, out_width), BF16),
        scratch_shapes=scratch,
        compiler_params=_params(("parallel", "arbitrary"), blocks, scratch_bytes),
        name=name,
    )(*q_arrays, *k_arrays, vt, *extra)


def _positions(seq):
    lp = seq + TK
    pos = jnp.zeros((lp,), jnp.int32)
    pos = pos.at[:seq].set(N_META + jnp.arange(seq, dtype=jnp.int32))
    return pos.at[seq:seq + N_META].set(jnp.arange(N_META, dtype=jnp.int32))


def _rope_tables(seq):
    pos = _positions(seq).astype(F32)[:, None]

    def cos_sin(d):
        half = d // 2
        inv = ROPE_THETA ** (-jnp.arange(half, dtype=F32) * (2.0 / d))
        ang = pos * inv[None, :]
        return jnp.cos(ang), jnp.sin(ang)

    c, s = cos_sin(HEAD_DIM)
    c2, s2 = cos_sin(IDX_DIM)
    z = jnp.zeros_like(c2)
    cat = lambda *parts: jnp.concatenate(parts, axis=1)
    return cat(cat(c, c), cat(-s, s),
               cat(c2, c2, c2, c2), cat(-s2, z, -s2, z), cat(z, s2, z, s2),
               cat(c2, c2, z, z), cat(-s2, z, z, z), cat(z, s2, z, z))


def _bias_table():
    k = lax.broadcasted_iota(jnp.int32, (TK, TQ), 0)
    q = lax.broadcasted_iota(jnp.int32, (TK, TQ), 1)
    is_meta = k < N_META
    visible = jnp.stack([
        is_meta,
        jnp.ones((TK, TQ), bool),
        k <= q,
        is_meta & ((k <= q) | (q >= N_META)),
    ])
    return jnp.where(visible, 0.0, NEG).astype(F32)


def _pack_w_in(w):
    z = lambda n: jnp.zeros((w.shape[0], n), w.dtype)
    return jnp.concatenate([
        w[:, 0:1024],
        w[:, 1088:4160],
        w[:, 5264:8336],
        w[:, 4160:5184],
        w[:, 8336:14480],
        w[:, 1024:1088], z(64),
        w[:, 5184:5264], z(48),
    ], axis=1).astype(BF16)


def _pack_mla_weights(w_q_up, w_kv_up, qk_g):
    wq = w_q_up.reshape(MLA_RANK, MLA_HEADS, MLA_QK)
    wq = jnp.pad(wq, ((0, 0), (0, 0), (0, MLA_HEAD_PAD - MLA_QK)))
    wq = wq.reshape(MLA_RANK, MLA_HEADS * MLA_HEAD_PAD).astype(BF16)
    wkv = w_kv_up.reshape(MLA_RANK, MLA_HEADS, MLA_NOPE + MLA_V)
    wkv = jnp.concatenate([wkv[:, :, :MLA_NOPE].reshape(MLA_RANK, -1),
                           wkv[:, :, MLA_NOPE:].reshape(MLA_RANK, -1)], axis=1).astype(BF16)
    gqk = jnp.pad(qk_g, ((0, 0), (0, MLA_HEAD_PAD - MLA_QK)))
    return wq, wkv, gqk


def kernel(x, meta, ln1_g, w_in, b_gate, mla_q_norm_g, mla_kv_norm_g, w_mla_q_up, w_mla_kv_up,
           mla_qk_g, dsa_qk_g, diff_qk_g, diff_lambda, diff_subln_g, w_branch, w_o, ln2_g,
           w_ffn_in, w_ffn_out):
    bsz, seq, d = x.shape
    assert d == D_MODEL and seq % TQ == 0
    depth = w_in.shape[0]
    nxb = seq // TQ
    nq = nxb + 1
    lp = seq + TK
    topk = min(TOPK_MAX, seq // 4)
    tm_dense = 1024 if (bsz * lp) % 1024 == 0 else 256
    tf = 512

    h = jnp.concatenate(
        [x, jnp.broadcast_to(meta[None].astype(x.dtype), (bsz, N_META, d)),
         jnp.zeros((bsz, lp - seq - N_META, d), x.dtype)], axis=1).reshape(bsz * lp, d)
    rope = _rope_tables(seq)
    btab = _bias_table()
    row = lambda v: v.reshape(1, -1)
    q_block = lambda a: (a, pl.BlockSpec((TQ, a.shape[1]), lambda b, i: (b * nq + i, 0)))

    for layer in range(depth):
        lambda_init = 0.8 - 0.6 * math.exp(-0.3 * layer)
        y = _norm_matmul(h, row(ln1_g[layer]), _pack_w_in(w_in[layer]), tm_dense, 768)

        wq, wkv, gqk = _pack_mla_weights(w_mla_q_up[layer], w_mla_kv_up[layer], mla_qk_g[layer])
        qm, km, vtm = _prep_mla(y, rope, row(mla_q_norm_g[layer]), row(mla_kv_norm_g[layer]),
                                wq, wkv, gqk, lp)
        qd, kd, vtd, iq, ikz, iwt = _prep_dsa(y, rope, dsa_qk_g[layer], lp)
        qf, kf, vtf = _prep_diff(y, rope, diff_qk_g[layer], lp)

        o_a = _attention_call(
            functools.partial(_mla_attn_kernel, nxb=nxb), "attn_mla", bsz, nq, lp,
            [q_block(qm)], [km], vtm, [btab], [], BRANCH_WIDTH, (MLA_HEADS, MLA_V, TQ))
        iwt_block = (iwt, pl.BlockSpec((IDX_HEADS, TQ), lambda b, i: (0, b * nq + i)))
        o_b = _attention_call(
            functools.partial(_dsa_attn_kernel, nxb=nxb, topk=topk), "attn_dsa", bsz, nq, lp,
            [q_block(qd), q_block(iq), iwt_block], [kd, ikz], vtd, [btab], [], BRANCH_WIDTH,
            (DSA_HEADS, HEAD_DIM, TQ), scratch_shapes=[pltpu.VMEM((nq, TK, TQ), F32)])
        o_c = _attention_call(
            functools.partial(_diff_attn_kernel, nxb=nxb, lambda_init=lambda_init), "attn_diff",
            bsz, nq, lp, [q_block(qf)], [kf], vtf,
            [btab, diff_lambda[layer], row(diff_subln_g[layer])],
            [pl.BlockSpec((4, HEAD_DIM), lambda b, i: (0, 0)),
             pl.BlockSpec((1, DIFF_V), lambda b, i: (0, 0))], BRANCH_WIDTH,
            (2 * DIFF_HEADS, DIFF_V, TQ))

        h = _merge_wo(h, o_a, o_b, o_c, y, row(b_gate[layer]), w_branch[layer].astype(BF16),
                      w_o[layer].astype(BF16), TQ)
        h = _ffn(h, row(ln2_g[layer]), w_ffn_in[layer].astype(BF16),
                 w_ffn_out[layer].astype(BF16), tm_dense, tf)

    return h.reshape(bsz, lp, d)[:, :seq]
```

```python
import functools
import math

import jax
import jax.numpy as jnp
from jax import lax
from jax.experimental import pallas as pl
from jax.experimental.pallas import tpu as pltpu

F32 = jnp.float32
BF16 = jnp.bfloat16

D_MODEL = 2048
N_META = 16
ROPE_THETA = 10000.0
EPS = 1e-6
TOPK_MAX = 256
MLA_HEADS = 8
MLA_RANK = 512
MLA_NOPE = 128
MLA_ROPE = 64
MLA_V = 128
MLA_QK = MLA_NOPE + MLA_ROPE
DSA_HEADS = 8
IDX_HEADS = 16
IDX_DIM = 64
DIFF_HEADS = 4
DIFF_V = 256
HEAD_DIM = 128
N_BRANCH = 3
BRANCH_WIDTH = 1024
D_FF = 5632

LANE = 128
V7X_VMEM_BYTES = 64 * 1024 * 1024
VMEM_COMPILER_RESERVE = 6 * 1024 * 1024

TQ = 256
TK = 256
MLA_HEAD_PAD = 256
NEG = -1e30
BISECT_STEPS = 16

COL_CQ, COL_CKV, COL_DQ, COL_DK, COL_DV = 0, 512, 1024, 2048, 3072
COL_FQ, COL_FK, COL_FV, COL_IQ, COL_G = 4096, 5120, 6144, 7168, 8192
COL_KPE, COL_IKW, D_IN_PACKED = 14336, 14464, 14592


def _vmem_limit(block_bytes, scratch_bytes=0):
    need = 2 * sum(block_bytes) + scratch_bytes + VMEM_COMPILER_RESERVE
    return int(min(need, V7X_VMEM_BYTES - 2 * 1024 * 1024))


def _nbytes(shape, dtype):
    return math.prod(shape) * jnp.dtype(dtype).itemsize


def _params(sem, block_bytes, scratch_bytes=0):
    return pltpu.CompilerParams(dimension_semantics=sem,
                                vmem_limit_bytes=_vmem_limit(block_bytes, scratch_bytes))


def _rms(x, g):
    ms = jnp.mean(x * x, axis=-1, keepdims=True)
    return x * lax.rsqrt(ms + EPS) * g


def _sigmoid(x):
    return 1.0 / (1.0 + jnp.exp(-x))


def _norm_matmul_kernel(x_ref, g_ref, w_ref, o_ref, xn_ref):
    @pl.when(pl.program_id(1) == 0)
    def _():
        xn_ref[...] = _rms(x_ref[...], g_ref[...]).astype(BF16)

    o_ref[...] = jnp.dot(xn_ref[...], w_ref[...], preferred_element_type=F32)


def _norm_matmul(x, g, w, tm, tn):
    t, d = x.shape
    n = w.shape[1]
    blocks = [_nbytes((tm, d), F32), _nbytes((d, tn), BF16), _nbytes((tm, tn), F32)]
    return pl.pallas_call(
        _norm_matmul_kernel,
        grid=(t // tm, n // tn),
        in_specs=[pl.BlockSpec((tm, d), lambda i, j: (i, 0)),
                  pl.BlockSpec((1, d), lambda i, j: (0, 0)),
                  pl.BlockSpec((d, tn), lambda i, j: (0, j))],
        out_specs=pl.BlockSpec((tm, tn), lambda i, j: (i, j)),
        out_shape=jax.ShapeDtypeStruct((t, n), F32),
        scratch_shapes=[pltpu.VMEM((tm, d), BF16)],
        compiler_params=_params(("parallel", "arbitrary"), blocks, _nbytes((tm, d), BF16)),
        name="inproj",
    )(x, g, w)


def _ffn_kernel(h_ref, g_ref, wa_ref, wb_ref, w2_ref, o_ref, xn_ref):
    @pl.when(pl.program_id(1) == 0)
    def _():
        h = h_ref[...]
        xn_ref[...] = _rms(h, g_ref[...]).astype(BF16)
        o_ref[...] = h

    xn = xn_ref[...]
    a = jnp.dot(xn, wa_ref[...], preferred_element_type=F32)
    act = (a * _sigmoid(a)) * jnp.dot(xn, wb_ref[...], preferred_element_type=F32)
    o_ref[...] += jnp.dot(act.astype(BF16), w2_ref[...], preferred_element_type=F32)


def _ffn(h, g, w1, w2, tm, tf):
    t, d = h.shape
    nf = w2.shape[0] // tf
    blocks = [_nbytes((tm, d), F32), 2 * _nbytes((d, tf), BF16), _nbytes((tf, d), BF16),
              _nbytes((tm, d), F32)]
    live = _nbytes((tm, d), BF16) + 2 * _nbytes((tm, tf), F32) + _nbytes((tm, d), F32)
    return pl.pallas_call(
        _ffn_kernel,
        grid=(t // tm, nf),
        in_specs=[pl.BlockSpec((tm, d), lambda i, f: (i, 0)),
                  pl.BlockSpec((1, d), lambda i, f: (0, 0)),
                  pl.BlockSpec((d, tf), lambda i, f: (0, f)),
                  pl.BlockSpec((d, tf), lambda i, f: (0, nf + f)),
                  pl.BlockSpec((tf, d), lambda i, f: (f, 0))],
        out_specs=pl.BlockSpec((tm, d), lambda i, f: (i, 0)),
        out_shape=jax.ShapeDtypeStruct((t, d), F32),
        scratch_shapes=[pltpu.VMEM((tm, d), BF16)],
        compiler_params=_params(("parallel", "arbitrary"), blocks, live),
        name="ffn",
    )(h, g, w1, w1, w2)


def _merge_wo_kernel(h_ref, oa_ref, ob_ref, oc_ref, ga_ref, gb_ref, gc_ref, bg_ref, wbr_ref,
                     wo_ref, o_ref):
    d = h_ref.shape[1]

    def branch(k, o_branch_ref, g_ref):
        gate = _sigmoid(g_ref[...] + bg_ref[:, k * d:(k + 1) * d])
        return gate * jnp.dot(o_branch_ref[...], wbr_ref[k], preferred_element_type=F32)

    merged = branch(0, oa_ref, ga_ref) + branch(1, ob_ref, gb_ref) + branch(2, oc_ref, gc_ref)
    o_ref[...] = h_ref[...] + jnp.dot(merged.astype(BF16), wo_ref[...],
                                      preferred_element_type=F32)


def _merge_wo(h, o_a, o_b, o_c, y, b_gate, w_branch, w_o, tm):
    t, d = h.shape
    gate_block0 = COL_G // d
    resident = pl.Buffered(1)
    blocks = [2 * _nbytes((tm, d), F32), 3 * _nbytes((tm, BRANCH_WIDTH), BF16),
              3 * _nbytes((tm, d), F32)]
    weights = _nbytes((N_BRANCH, BRANCH_WIDTH, d), BF16) + _nbytes((d, d), BF16)
    o_spec = pl.BlockSpec((tm, BRANCH_WIDTH), lambda i: (i, 0))
    g_spec = lambda k: pl.BlockSpec((tm, d), lambda i: (i, gate_block0 + k))
    return pl.pallas_call(
        _merge_wo_kernel,
        grid=(t // tm,),
        in_specs=[pl.BlockSpec((tm, d), lambda i: (i, 0)), o_spec, o_spec, o_spec,
                  g_spec(0), g_spec(1), g_spec(2),
                  pl.BlockSpec((1, N_BRANCH * d), lambda i: (0, 0)),
                  pl.BlockSpec((N_BRANCH, BRANCH_WIDTH, d), lambda i: (0, 0, 0),
                               pipeline_mode=resident),
                  pl.BlockSpec((d, d), lambda i: (0, 0), pipeline_mode=resident)],
        out_specs=pl.BlockSpec((tm, d), lambda i: (i, 0)),
        out_shape=jax.ShapeDtypeStruct((t, d), F32),
        compiler_params=_params(("parallel",), blocks, weights + 4 * _nbytes((tm, d), F32)),
        name="merge_wo",
    )(h, o_a, o_b, o_c, y, y, y, b_gate, w_branch, w_o)


ROPE_COS, ROPE_SIN, ROPE_C2, ROPE_SA2, ROPE_SB2, ROPE_C1, ROPE_SA1, ROPE_SB1 = (
    k * LANE for k in range(8))
ROPE_TABLE_WIDTH = 8 * LANE


def _tab(rope_ref, off):
    return rope_ref[:, off:off + LANE]


def _rope128(y, rope_ref):
    return y * _tab(rope_ref, ROPE_COS) + pltpu.roll(y, 64, 1) * _tab(rope_ref, ROPE_SIN)


def _rope64(y, rope_ref, c, sa, sb):
    return (y * _tab(rope_ref, c) + pltpu.roll(y, 96, 1) * _tab(rope_ref, sa)
            + pltpu.roll(y, 32, 1) * _tab(rope_ref, sb))


def _norm_rope_heads(x_ref, g, rope_ref, o_ref):
    for h in range(x_ref.shape[1] // HEAD_DIM):
        sl = slice(h * HEAD_DIM, (h + 1) * HEAD_DIM)
        o_ref[:, sl] = _rope128(_rms(x_ref[:, sl], g), rope_ref).astype(BF16)


def _store_transposed(v, vt_ref):
    for b in range(v.shape[1] // LANE):
        sl = slice(b * LANE, (b + 1) * LANE)
        vt_ref[0, sl, :] = v[:, sl].T.astype(BF16)


def _prep_dsa_kernel(dq_ref, dk_ref, dv_ref, iq_ref, ikw_ref, rope_ref, g_ref,
                     qo_ref, ko_ref, vto_ref, iqo_ref, ikzo_ref, iwto_ref):
    _norm_rope_heads(dq_ref, g_ref[0:1, :], rope_ref, qo_ref)
    _norm_rope_heads(dk_ref, g_ref[1:2, :], rope_ref, ko_ref)
    _store_transposed(dv_ref[...], vto_ref)
    for p in range(IDX_HEADS // 2):
        sl = slice(p * LANE, (p + 1) * LANE)
        iqo_ref[:, sl] = _rope64(iq_ref[:, sl], rope_ref, ROPE_C2, ROPE_SA2, ROPE_SB2).astype(BF16)
    x = ikw_ref[...]
    ik = _rope64(x, rope_ref, ROPE_C1, ROPE_SA1, ROPE_SB1)
    ikzo_ref[:, 0:LANE] = ik.astype(BF16)
    ikzo_ref[:, LANE:2 * LANE] = pltpu.roll(ik, 64, 1).astype(BF16)
    w = pltpu.roll(x, 64, 1) * (IDX_HEADS ** -0.5 * IDX_DIM ** -0.5)
    iwto_ref[...] = w.T[0:IDX_HEADS, :]


def _prep_diff_kernel(fq_ref, fk_ref, fv_ref, rope_ref, g_ref, qo_ref, ko_ref, vto_ref):
    _norm_rope_heads(fq_ref, g_ref[0:1, :], rope_ref, qo_ref)
    _norm_rope_heads(fk_ref, g_ref[1:2, :], rope_ref, ko_ref)
    _store_transposed(fv_ref[...], vto_ref)


def _prep_mla_kernel(cq_ref, ckv_ref, kpe_ref, rope_ref, qg_ref, kvg_ref, wq_ref, wkv_ref,
                     gqk_ref, qo_ref, ko_ref, vto_ref):
    inv_qk = 1.0 / MLA_QK
    rope1 = functools.partial(_rope64, rope_ref=rope_ref, c=ROPE_C1, sa=ROPE_SA1, sb=ROPE_SB1)

    def ssq(v):
        return jnp.sum(v * v, axis=-1, keepdims=True)

    cq = _rms(cq_ref[...], qg_ref[...]).astype(BF16)
    q = jnp.dot(cq, wq_ref[...], preferred_element_type=F32)
    gq0, gq1 = gqk_ref[0:1, 0:LANE], gqk_ref[0:1, LANE:2 * LANE]
    gk0, gk1 = gqk_ref[1:2, 0:LANE], gqk_ref[1:2, LANE:2 * LANE]
    for h in range(MLA_HEADS):
        c0 = h * MLA_HEAD_PAD
        b0, b1 = q[:, c0:c0 + LANE], q[:, c0 + LANE:c0 + 2 * LANE]
        r = lax.rsqrt(jnp.sum(b0 * b0 + b1 * b1, axis=-1, keepdims=True) * inv_qk + EPS)
        qo_ref[:, c0:c0 + LANE] = (b0 * r * gq0).astype(BF16)
        qo_ref[:, c0 + LANE:c0 + 2 * LANE] = rope1(b1 * r * gq1).astype(BF16)

    ckv = _rms(ckv_ref[...], kvg_ref[...]).astype(BF16)
    kv = jnp.dot(ckv, wkv_ref[...], preferred_element_type=F32)
    nk = MLA_HEADS * MLA_NOPE
    _store_transposed(kv[:, nk:], vto_ref)
    kp = kpe_ref[...]
    skp = ssq(kp)
    kp_rot = rope1(kp * gk1)
    for h in range(MLA_HEADS):
        c0 = h * MLA_HEAD_PAD
        kn = kv[:, h * MLA_NOPE:(h + 1) * MLA_NOPE]
        r = lax.rsqrt((ssq(kn) + skp) * inv_qk + EPS)
        ko_ref[:, c0:c0 + LANE] = (kn * r * gk0).astype(BF16)
        ko_ref[:, c0 + LANE:c0 + 2 * LANE] = (kp_rot * r).astype(BF16)


def _row_spec(tm, width, col_block):
    return pl.BlockSpec((tm, width), lambda i: (i, col_block))


def _full_spec(shape):
    return pl.BlockSpec(shape, lambda i: (0,) * len(shape))


def _rope_spec(tm, lp):
    nblk = lp // tm
    return pl.BlockSpec((tm, ROPE_TABLE_WIDTH), lambda i: (i % nblk, 0))


def _vt_out(t, width):
    return (pl.BlockSpec((1, width, TK), lambda i: (i, 0, 0)),
            jax.ShapeDtypeStruct((t // TK, width, TK), BF16))


def _prep_dsa(y, rope, g, lp):
    t = y.shape[0]
    tm = TK
    w = BRANCH_WIDTH
    blocks = [4 * _nbytes((tm, w), F32), _nbytes((tm, LANE), F32),
              _nbytes((tm, ROPE_TABLE_WIDTH), F32), 4 * _nbytes((tm, w), BF16),
              _nbytes((tm, 2 * LANE), BF16), _nbytes((IDX_HEADS, tm), F32)]
    out_w = lambda width: pl.BlockSpec((tm, width), lambda i: (i, 0))
    vt_spec, vt_shape = _vt_out(t, w)
    return pl.pallas_call(
        _prep_dsa_kernel,
        grid=(t // tm,),
        in_specs=[_row_spec(tm, w, COL_DQ // w), _row_spec(tm, w, COL_DK // w),
                  _row_spec(tm, w, COL_DV // w), _row_spec(tm, w, COL_IQ // w),
                  _row_spec(tm, LANE, COL_IKW // LANE), _rope_spec(tm, lp),
                  _full_spec((2, HEAD_DIM))],
        out_specs=[out_w(w), out_w(w), vt_spec, out_w(w), out_w(2 * LANE),
                   pl.BlockSpec((IDX_HEADS, tm), lambda i: (0, i))],
        out_shape=[jax.ShapeDtypeStruct((t, w), BF16), jax.ShapeDtypeStruct((t, w), BF16),
                   vt_shape, jax.ShapeDtypeStruct((t, w), BF16),
                   jax.ShapeDtypeStruct((t, 2 * LANE), BF16),
                   jax.ShapeDtypeStruct((IDX_HEADS, t), F32)],
        compiler_params=_params(("parallel",), blocks),
        name="prep_dsa",
    )(y, y, y, y, y, rope, g)


def _prep_diff(y, rope, g, lp):
    t = y.shape[0]
    tm = TK
    w = BRANCH_WIDTH
    blocks = [3 * _nbytes((tm, w), F32), _nbytes((tm, ROPE_TABLE_WIDTH), F32),
              3 * _nbytes((tm, w), BF16)]
    out = pl.BlockSpec((tm, w), lambda i: (i, 0))
    vt_spec, vt_shape = _vt_out(t, w)
    return pl.pallas_call(
        _prep_diff_kernel,
        grid=(t // tm,),
        in_specs=[_row_spec(tm, w, COL_FQ // w), _row_spec(tm, w, COL_FK // w),
                  _row_spec(tm, w, COL_FV // w), _rope_spec(tm, lp), _full_spec((2, HEAD_DIM))],
        out_specs=[out, out, vt_spec],
        out_shape=[jax.ShapeDtypeStruct((t, w), BF16), jax.ShapeDtypeStruct((t, w), BF16),
                   vt_shape],
        compiler_params=_params(("parallel",), blocks),
        name="prep_diff",
    )(y, y, y, rope, g)


def _prep_mla(y, rope, q_norm_g, kv_norm_g, wq, wkv, gqk, lp):
    t = y.shape[0]
    tm = TK
    r = MLA_RANK
    qkw = MLA_HEADS * MLA_HEAD_PAD
    vw = MLA_HEADS * MLA_V
    blocks = [2 * _nbytes((tm, r), F32), _nbytes((tm, LANE), F32),
              _nbytes((tm, ROPE_TABLE_WIDTH), F32), _nbytes((r, qkw), BF16),
              _nbytes((r, 2 * vw), BF16), 2 * _nbytes((tm, qkw), BF16), _nbytes((tm, vw), BF16)]
    scratch = 2 * _nbytes((tm, qkw), F32)
    out_w = lambda width: pl.BlockSpec((tm, width), lambda i: (i, 0))
    vt_spec, vt_shape = _vt_out(t, vw)
    return pl.pallas_call(
        _prep_mla_kernel,
        grid=(t // tm,),
        in_specs=[_row_spec(tm, r, COL_CQ // r), _row_spec(tm, r, COL_CKV // r),
                  _row_spec(tm, LANE, COL_KPE // LANE), _rope_spec(tm, lp),
                  _full_spec((1, r)), _full_spec((1, r)), _full_spec((r, qkw)),
                  _full_spec((r, 2 * vw)), _full_spec((2, MLA_HEAD_PAD))],
        out_specs=[out_w(qkw), out_w(qkw), vt_spec],
        out_shape=[jax.ShapeDtypeStruct((t, qkw), BF16), jax.ShapeDtypeStruct((t, qkw), BF16),
                   vt_shape],
        compiler_params=_params(("parallel",), blocks, scratch),
        name="prep_mla",
    )(y, y, y, rope, q_norm_g, kv_norm_g, wq, wkv, gqk)


def _prep_all_kernel(cq_ref, ckv_ref, kpe_ref, dq_ref, dk_ref, dv_ref, iq_ref, ikw_ref, fq_ref,
                     fk_ref, fv_ref, rope_ref, qg_ref, kvg_ref, wq_ref, wkv_ref, gqk_ref, gd_ref,
                     gf_ref, qm_ref, km_ref, vtm_ref, qd_ref, kd_ref, vtd_ref, iqo_ref, ikzo_ref,
                     iwto_ref, qf_ref, kf_ref, vtf_ref):
    _prep_mla_kernel(cq_ref, ckv_ref, kpe_ref, rope_ref, qg_ref, kvg_ref, wq_ref, wkv_ref,
                     gqk_ref, qm_ref, km_ref, vtm_ref)
    _prep_dsa_kernel(dq_ref, dk_ref, dv_ref, iq_ref, ikw_ref, rope_ref, gd_ref,
                     qd_ref, kd_ref, vtd_ref, iqo_ref, ikzo_ref, iwto_ref)
    _prep_diff_kernel(fq_ref, fk_ref, fv_ref, rope_ref, gf_ref, qf_ref, kf_ref, vtf_ref)


def _prep_all(y, rope, q_norm_g, kv_norm_g, wq, wkv, gqk, g_dsa, g_diff, lp):
    t = y.shape[0]
    tm = TK
    r = MLA_RANK
    w = BRANCH_WIDTH
    qkw = MLA_HEADS * MLA_HEAD_PAD
    blocks = [2 * _nbytes((tm, r), F32), 2 * _nbytes((tm, LANE), F32), 7 * _nbytes((tm, w), F32),
              _nbytes((tm, ROPE_TABLE_WIDTH), F32), _nbytes((r, qkw), BF16),
              _nbytes((r, 2 * w), BF16), 2 * _nbytes((tm, qkw), BF16), 8 * _nbytes((tm, w), BF16),
              _nbytes((tm, 2 * LANE), BF16), _nbytes((IDX_HEADS, tm), F32)]
    scratch = 2 * _nbytes((tm, qkw), F32)
    out_w = lambda width: pl.BlockSpec((tm, width), lambda i: (i, 0))
    tok = lambda width: jax.ShapeDtypeStruct((t, width), BF16)
    vt_spec, vt_shape = _vt_out(t, w)
    return pl.pallas_call(
        _prep_all_kernel,
        grid=(t // tm,),
        in_specs=[_row_spec(tm, r, COL_CQ // r), _row_spec(tm, r, COL_CKV // r),
                  _row_spec(tm, LANE, COL_KPE // LANE),
                  _row_spec(tm, w, COL_DQ // w), _row_spec(tm, w, COL_DK // w),
                  _row_spec(tm, w, COL_DV // w), _row_spec(tm, w, COL_IQ // w),
                  _row_spec(tm, LANE, COL_IKW // LANE),
                  _row_spec(tm, w, COL_FQ // w), _row_spec(tm, w, COL_FK // w),
                  _row_spec(tm, w, COL_FV // w), _rope_spec(tm, lp),
                  _full_spec((1, r)), _full_spec((1, r)), _full_spec((r, qkw)),
                  _full_spec((r, 2 * w)), _full_spec((2, MLA_HEAD_PAD)),
                  _full_spec((2, HEAD_DIM)), _full_spec((2, HEAD_DIM))],
        out_specs=[out_w(qkw), out_w(qkw), vt_spec, out_w(w), out_w(w), vt_spec, out_w(w),
                   out_w(2 * LANE), pl.BlockSpec((IDX_HEADS, tm), lambda i: (0, i)),
                   out_w(w), out_w(w), vt_spec],
        out_shape=[tok(qkw), tok(qkw), vt_shape, tok(w), tok(w), vt_shape, tok(w),
                   tok(2 * LANE), jax.ShapeDtypeStruct((IDX_HEADS, t), F32),
                   tok(w), tok(w), vt_shape],
        compiler_params=_params(("parallel",), blocks, scratch),
        name="prep",
    )(y, y, y, y, y, y, y, y, y, y, y, rope, q_norm_g, kv_norm_g, wq, wkv, gqk, g_dsa, g_diff)


BIAS_META, BIAS_NONE, BIAS_DIAG, BIAS_META_Q = 0, 1, 2, 3
LOG2E = math.log2(math.e)


def _schedule(i, nxb):
    meta_q = i == nxb
    n_steps = jnp.where(meta_q, 1, i + 2)

    def step(c):
        chunk = jnp.where(c == 0, nxb, c - 1)
        kind = jnp.where(meta_q, BIAS_META_Q,
                         jnp.where(c == 0, BIAS_META, jnp.where(c - 1 == i, BIAS_DIAG, BIAS_NONE)))
        return chunk, kind

    return n_steps, step


def _chunk_rows(chunk):
    return pl.ds(pl.multiple_of(chunk * TK, TK), TK)


def _kq(k, q):
    return lax.dot_general(k, q, (((1,), (1,)), ((), ())), preferred_element_type=F32)


def _fold_rows(x, op):
    parts = [x[r:r + 8, :] for r in range(0, x.shape[0], 8)]
    while len(parts) > 1:
        parts = [op(parts[j], parts[j + 1]) for j in range(0, len(parts) - 1, 2)] + (
            [parts[-1]] if len(parts) % 2 else [])
    return parts[0]


def _flash_heads(q_ref, k_ref, vt_ref, heads, scale, n_steps, step, bias_fn, acc_ref, s_ref):
    nh = len(heads)
    scale2 = scale * LOG2E
    acc_ref[...] = jnp.zeros_like(acc_ref)

    def scores(c):
        chunk, kind = step(c)
        rows = _chunk_rows(chunk)
        bias = bias_fn(c, kind)
        for h, (qk_cols, _) in enumerate(heads):
            s_ref[h] = _kq(k_ref[rows, qk_cols], q_ref[:, qk_cols]) * scale2 + bias

    scores(0)

    def body(c, carry):
        ms, ls = carry
        chunk, _ = step(c)
        new_ms, new_ls = [], []
        for h, (_, v_rows) in enumerate(heads):
            s = s_ref[h]
            m_new = jnp.maximum(ms[h], jnp.max(s, axis=0, keepdims=True))
            alpha = jnp.exp2(ms[h] - m_new)
            p = jnp.exp2(s - m_new)
            new_ms.append(m_new)
            new_ls.append(alpha * ls[h] + jnp.sum(p, axis=0, keepdims=True))
            pv = jnp.dot(vt_ref[chunk, v_rows, :], p.astype(BF16), preferred_element_type=F32)
            acc_ref[h] = alpha * acc_ref[h] + pv
        scores(jnp.minimum(c + 1, n_steps - 1))
        return tuple(new_ms), tuple(new_ls)

    init = (tuple(jnp.full((1, TQ), NEG, F32) for _ in range(nh)),
            tuple(jnp.zeros((1, TQ), F32) for _ in range(nh)))
    _, ls = lax.fori_loop(0, n_steps, body, init)
    return [acc_ref[h] / ls[h] for h in range(nh)]


def _mla_attn_kernel(q_ref, k_ref, vt_ref, btab_ref, o_ref, acc_ref, s_ref, *, nxb):
    n_steps, step = _schedule(pl.program_id(1), nxb)
    heads = [(slice(h * MLA_HEAD_PAD, (h + 1) * MLA_HEAD_PAD), slice(h * MLA_V, (h + 1) * MLA_V))
             for h in range(MLA_HEADS)]
    outs = _flash_heads(q_ref, k_ref, vt_ref, heads, MLA_QK ** -0.5, n_steps, step,
                        lambda c, kind: btab_ref[kind], acc_ref, s_ref)
    for h, o in enumerate(outs):
        o_ref[:, h * MLA_V:(h + 1) * MLA_V] = o.T.astype(o_ref.dtype)


def _diff_attn_kernel(q_ref, k_ref, vt_ref, btab_ref, lam_ref, subg_ref, o_ref, acc_ref, s_ref,
                      *, nxb, lambda_init):
    n_steps, step = _schedule(pl.program_id(1), nxb)
    lv = lam_ref[...]
    dot01 = jnp.sum(lv[0:1, :] * lv[1:2, :], axis=-1, keepdims=True)
    dot23 = jnp.sum(lv[2:3, :] * lv[3:4, :], axis=-1, keepdims=True)
    lam = jnp.exp(dot01) - jnp.exp(dot23) + lambda_init
    heads = [(slice(j * HEAD_DIM, (j + 1) * HEAD_DIM),
              slice((j // 2) * DIFF_V, (j // 2 + 1) * DIFF_V)) for j in range(2 * DIFF_HEADS)]
    maps = _flash_heads(q_ref, k_ref, vt_ref, heads, HEAD_DIM ** -0.5, n_steps, step,
                        lambda c, kind: btab_ref[kind], acc_ref, s_ref)
    for h in range(DIFF_HEADS):
        o = (maps[2 * h] - lam * maps[2 * h + 1]).T
        o_ref[:, h * DIFF_V:(h + 1) * DIFF_V] = (
            _rms(o, subg_ref[...]) * (1.0 - lambda_init)).astype(o_ref.dtype)


def _dsa_attn_kernel(q_ref, iq_ref, iwt_ref, k_ref, ikz_ref, vt_ref, btab_ref, o_ref,
                     acc_ref, s_ref, sc_ref, scm_ref, *, nxb, topk):
    n_steps, step = _schedule(pl.program_id(1), nxb)
    kf = float(topk)
    inf = jnp.inf
    seq = nxb * TK

    def index_scores(rows):
        acc = None
        for p in range(IDX_HEADS // 2):
            iq_pair = iq_ref[:, p * LANE:(p + 1) * LANE]
            for e in range(2):
                logits = _kq(ikz_ref[rows, e * LANE:(e + 1) * LANE], iq_pair)
                term = jnp.maximum(logits, 0.0) * iwt_ref[2 * p + e:2 * p + e + 1, :]
                acc = term if acc is None else acc + term
        return acc

    meta_mask = btab_ref[step(0)[1]][0:N_META, :]
    scm_ref[...] = jnp.where(meta_mask == 0.0, index_scores(slice(seq, seq + N_META)), -inf)

    def score_step(c, carry):
        chunk, kind = step(c)
        sc_ref[c] = jnp.where(btab_ref[kind] == 0.0, index_scores(_chunk_rows(chunk)), -inf)
        return carry

    lax.fori_loop(1, n_steps, score_step, 0)

    def reduce_tiles(tile_fn, op, init):
        part = op(init, tile_fn(scm_ref[...]))
        return lax.fori_loop(1, n_steps, lambda c, acc: op(acc, tile_fn(sc_ref[c])), part)

    def count(pred):
        part = reduce_tiles(lambda s: _fold_rows(jnp.where(pred(s), 1.0, 0.0), jnp.add),
                            jnp.add, jnp.zeros((8, TQ), F32))
        return jnp.sum(part, axis=0, keepdims=True)

    def min_where(pred):
        part = reduce_tiles(lambda s: _fold_rows(jnp.where(pred(s), s, inf), jnp.minimum),
                            jnp.minimum, jnp.full((8, TQ), inf, F32))
        return jnp.min(part, axis=0, keepdims=True)

    mx = jnp.max(reduce_tiles(lambda s: _fold_rows(s, jnp.maximum), jnp.maximum,
                              jnp.full((8, TQ), -inf, F32)), axis=0, keepdims=True)
    mn = min_where(lambda s: s > -inf)
    nv = count(lambda s: s > -inf)

    def bisect(_, carry):
        lo, hi = carry
        mid = 0.5 * (lo + hi)
        ge = count(lambda s: s >= mid) >= kf
        return jnp.where(ge, mid, lo), jnp.where(ge, hi, mid)

    lo, _ = lax.fori_loop(0, BISECT_STEPS, bisect, (mn, mx))
    t = min_where(lambda s: s >= lo)
    n_above = count(lambda s: s > t)

    def unsettled(state):
        return jnp.max(jnp.where(state[1] >= kf, 1.0, 0.0)) > 0.0

    def raise_threshold(state):
        t, n_above = state
        t = jnp.where(n_above >= kf, min_where(lambda s: s > t), t)
        return t, count(lambda s: s > t)

    t, n_above = lax.while_loop(unsettled, raise_threshold, (t, n_above))
    keep_all = nv <= kf
    t = jnp.where(keep_all, mn, t)
    n_above = jnp.where(keep_all, count(lambda s: s > mn), n_above)

    n_equal_taken = kf - n_above
    n_equal = count(lambda s: s >= t) - n_above
    pad_rows = jnp.full((TK - N_META, TQ), NEG, F32)

    def store_meta(sel):
        sc_ref[0] = jnp.concatenate([jnp.where(sel, 0.0, NEG), pad_rows], axis=0)

    def mask_plain():
        store_meta(scm_ref[...] >= t)

        def body(c, carry):
            sc_ref[c] = jnp.where(sc_ref[c] >= t, 0.0, NEG)
            return carry

        lax.fori_loop(1, n_steps, body, 0)

    def mask_with_ties():
        lower = (lax.broadcasted_iota(jnp.int32, (TK, TK), 1)
                 <= lax.broadcasted_iota(jnp.int32, (TK, TK), 0)).astype(BF16)

        def select(s, eq_f, prefix, seen):
            return (s > t) | ((eq_f > 0.0) & (seen + prefix <= n_equal_taken))

        s = scm_ref[...]
        eq_f = jnp.where(s == t, 1.0, 0.0)
        eq_pad = jnp.concatenate([eq_f, jnp.zeros((LANE - N_META, TQ), F32)], axis=0)
        prefix = jnp.dot(lower[0:LANE, 0:LANE], eq_pad.astype(BF16),
                         preferred_element_type=F32)[0:N_META, :]
        store_meta(select(s, eq_f, prefix, 0.0))

        def body(c, seen):
            s = sc_ref[c]
            eq_f = jnp.where(s == t, 1.0, 0.0)
            prefix = jnp.dot(lower, eq_f.astype(BF16), preferred_element_type=F32)
            sc_ref[c] = jnp.where(select(s, eq_f, prefix, seen), 0.0, NEG)
            return seen + jnp.sum(eq_f, axis=0, keepdims=True)

        lax.fori_loop(1, n_steps, body, jnp.sum(eq_f, axis=0, keepdims=True))

    surplus_ties = jnp.max(jnp.where(n_equal > n_equal_taken, 1.0, 0.0)) > 0.0
    lax.cond(surplus_ties, mask_with_ties, mask_plain)

    heads = [(slice(h * HEAD_DIM, (h + 1) * HEAD_DIM),) * 2 for h in range(DSA_HEADS)]
    outs = _flash_heads(q_ref, k_ref, vt_ref, heads, HEAD_DIM ** -0.5, n_steps, step,
                        lambda c, kind: sc_ref[c], acc_ref, s_ref)
    for h, o in enumerate(outs):
        o_ref[:, h * HEAD_DIM:(h + 1) * HEAD_DIM] = o.T.astype(o_ref.dtype)


def _attention_call(kernel, name, bsz, nq, lp, q_specs_arrays, k_arrays, vt, extra, extra_specs,
                    out_width, acc_shape, scratch_shapes=()):
    q_arrays = [a for a, _ in q_specs_arrays]
    q_specs = [s for _, s in q_specs_arrays]
    k_specs = [pl.BlockSpec((lp, a.shape[1]), lambda b, i: (b, 0)) for a in k_arrays]
    vt_spec = pl.BlockSpec((nq, vt.shape[1], TK), lambda b, i: (b, 0, 0))
    btab_spec = pl.BlockSpec((4, TK, TQ), lambda b, i: (0, 0, 0))
    scratch = [pltpu.VMEM(acc_shape, F32), pltpu.VMEM((acc_shape[0], TK, TQ), F32)] + list(
        scratch_shapes)
    blocks = ([_nbytes(s.block_shape, a.dtype) for a, s in q_specs_arrays]
              + [_nbytes((lp, a.shape[1]), a.dtype) for a in k_arrays]
              + [_nbytes((nq, vt.shape[1], TK), BF16), _nbytes((4, TK, TQ), F32),
                 _nbytes((TQ, out_width), BF16)])
    scratch_bytes = sum(_nbytes(s.shape, s.dtype) for s in scratch)
    return pl.pallas_call(
        kernel,
        grid=(bsz, nq),
        in_specs=q_specs + k_specs + [vt_spec, btab_spec] + extra_specs,
        out_specs=pl.BlockSpec((TQ, out_width), lambda b, i: (b * nq + i, 0)),
        out_shape=jax.ShapeDtypeStruct((bsz * lp, out_width), BF16),
        scratch_shapes=scratch,
        compiler_params=_params(("parallel", "arbitrary"), blocks, scratch_bytes),
        name=name,
    )(*q_arrays, *k_arrays, vt, *extra)


def _positions(seq):
    lp = seq + TK
    pos = jnp.zeros((lp,), jnp.int32)
    pos = pos.at[:seq].set(N_META + jnp.arange(seq, dtype=jnp.int32))
    return pos.at[seq:seq + N_META].set(jnp.arange(N_META, dtype=jnp.int32))


def _rope_tables(seq):
    pos = _positions(seq).astype(F32)[:, None]

    def cos_sin(d):
        half = d // 2
        inv = ROPE_THETA ** (-jnp.arange(half, dtype=F32) * (2.0 / d))
        ang = pos * inv[None, :]
        return jnp.cos(ang), jnp.sin(ang)

    c, s = cos_sin(HEAD_DIM)
    c2, s2 = cos_sin(IDX_DIM)
    z = jnp.zeros_like(c2)
    cat = lambda *parts: jnp.concatenate(parts, axis=1)
    return cat(cat(c, c), cat(-s, s),
               cat(c2, c2, c2, c2), cat(-s2, z, -s2, z), cat(z, s2, z, s2),
               cat(c2, c2, z, z), cat(-s2, z, z, z), cat(z, s2, z, z))


def _bias_table():
    k = lax.broadcasted_iota(jnp.int32, (TK, TQ), 0)
    q = lax.broadcasted_iota(jnp.int32, (TK, TQ), 1)
    is_meta = k < N_META
    visible = jnp.stack([
        is_meta,
        jnp.ones((TK, TQ), bool),
        k <= q,
        is_meta & ((k <= q) | (q >= N_META)),
    ])
    return jnp.where(visible, 0.0, NEG).astype(F32)


def _pack_w_in(w):
    z = lambda n: jnp.zeros((w.shape[0], n), w.dtype)
    return jnp.concatenate([
        w[:, 0:1024],
        w[:, 1088:4160],
        w[:, 5264:8336],
        w[:, 4160:5184],
        w[:, 8336:14480],
        w[:, 1024:1088], z(64),
        w[:, 5184:5264], z(48),
    ], axis=1).astype(BF16)


def _pack_mla_weights(w_q_up, w_kv_up, qk_g):
    wq = w_q_up.reshape(MLA_RANK, MLA_HEADS, MLA_QK)
    wq = jnp.pad(wq, ((0, 0), (0, 0), (0, MLA_HEAD_PAD - MLA_QK)))
    wq = wq.reshape(MLA_RANK, MLA_HEADS * MLA_HEAD_PAD).astype(BF16)
    wkv = w_kv_up.reshape(MLA_RANK, MLA_HEADS, MLA_NOPE + MLA_V)
    wkv = jnp.concatenate([wkv[:, :, :MLA_NOPE].reshape(MLA_RANK, -1),
                           wkv[:, :, MLA_NOPE:].reshape(MLA_RANK, -1)], axis=1).astype(BF16)
    gqk = jnp.pad(qk_g, ((0, 0), (0, MLA_HEAD_PAD - MLA_QK)))
    return wq, wkv, gqk


def kernel(x, meta, ln1_g, w_in, b_gate, mla_q_norm_g, mla_kv_norm_g, w_mla_q_up, w_mla_kv_up,
           mla_qk_g, dsa_qk_g, diff_qk_g, diff_lambda, diff_subln_g, w_branch, w_o, ln2_g,
           w_ffn_in, w_ffn_out):
    bsz, seq, d = x.shape
    assert d == D_MODEL and seq % TQ == 0
    depth = w_in.shape[0]
    nxb = seq // TQ
    nq = nxb + 1
    lp = seq + TK
    topk = min(TOPK_MAX, seq // 4)
    tm_dense = 1024 if (bsz * lp) % 1024 == 0 else 256
    tf = 512

    h = jnp.concatenate(
        [x, jnp.broadcast_to(meta[None].astype(x.dtype), (bsz, N_META, d)),
         jnp.zeros((bsz, lp - seq - N_META, d), x.dtype)], axis=1).reshape(bsz * lp, d)
    rope = _rope_tables(seq)
    btab = _bias_table()
    row = lambda v: v.reshape(1, -1)
    q_block = lambda a: (a, pl.BlockSpec((TQ, a.shape[1]), lambda b, i: (b * nq + i, 0)))

    for layer in range(depth):
        lambda_init = 0.8 - 0.6 * math.exp(-0.3 * layer)
        y = _norm_matmul(h, row(ln1_g[layer]), _pack_w_in(w_in[layer]), tm_dense, 768)

        wq, wkv, gqk = _pack_mla_weights(w_mla_q_up[layer], w_mla_kv_up[layer], mla_qk_g[layer])
        qm, km, vtm, qd, kd, vtd, iq, ikz, iwt, qf, kf, vtf = _prep_all(
            y, rope, row(mla_q_norm_g[layer]), row(mla_kv_norm_g[layer]), wq, wkv, gqk,
            dsa_qk_g[layer], diff_qk_g[layer], lp)

        o_a = _attention_call(
            functools.partial(_mla_attn_kernel, nxb=nxb), "attn_mla", bsz, nq, lp,
            [q_block(qm)], [km], vtm, [btab], [], BRANCH_WIDTH, (MLA_HEADS, MLA_V, TQ))
        iwt_block = (iwt, pl.BlockSpec((IDX_HEADS, TQ), lambda b, i: (0, b * nq + i)))
        o_b = _attention_call(
            functools.partial(_dsa_attn_kernel, nxb=nxb, topk=topk), "attn_dsa", bsz, nq, lp,
            [q_block(qd), q_block(iq), iwt_block], [kd, ikz], vtd, [btab], [], BRANCH_WIDTH,
            (DSA_HEADS, HEAD_DIM, TQ),
            scratch_shapes=[pltpu.VMEM((nq, TK, TQ), F32), pltpu.VMEM((N_META, TQ), F32)])
        o_c = _attention_call(
            functools.partial(_diff_attn_kernel, nxb=nxb, lambda_init=lambda_init), "attn_diff",
            bsz, nq, lp, [q_block(qf)], [kf], vtf,
            [btab, diff_lambda[layer], row(diff_subln_g[layer])],
            [pl.BlockSpec((4, HEAD_DIM), lambda b, i: (0, 0)),
             pl.BlockSpec((1, DIFF_V), lambda b, i: (0, 0))], BRANCH_WIDTH,
            (2 * DIFF_HEADS, DIFF_V, TQ))

        h = _merge_wo(h, o_a, o_b, o_c, y, row(b_gate[layer]), w_branch[layer].astype(BF16),
                      w_o[layer].astype(BF16), TQ)
        h = _ffn(h, row(ln2_g[layer]), w_ffn_in[layer].astype(BF16),
                 w_ffn_out[layer].astype(BF16), tm_dense, tf)

    return h.reshape(bsz, lp, d)[:, :seq]
```

```python
import functools
import math

import jax
import jax.numpy as jnp
from jax import lax
from jax.experimental import pallas as pl
from jax.experimental.pallas import tpu as pltpu

F32 = jnp.float32
BF16 = jnp.bfloat16

D_MODEL = 2048
N_META = 16
ROPE_THETA = 10000.0
EPS = 1e-6
TOPK_MAX = 256
MLA_HEADS = 8
MLA_RANK = 512
MLA_NOPE = 128
MLA_ROPE = 64
MLA_V = 128
MLA_QK = MLA_NOPE + MLA_ROPE
DSA_HEADS = 8
IDX_HEADS = 16
IDX_DIM = 64
DIFF_HEADS = 4
DIFF_V = 256
HEAD_DIM = 128
N_BRANCH = 3
BRANCH_WIDTH = 1024
D_FF = 5632

LANE = 128
V7X_VMEM_BYTES = 64 * 1024 * 1024
VMEM_COMPILER_RESERVE = 6 * 1024 * 1024

TQ = 256
TK = 256
MLA_HEAD_PAD = 256
NEG = -1e30
BISECT_STEPS = 16

COL_CQ, COL_CKV, COL_DQ, COL_DK, COL_FQ, COL_FK = 0, 512, 1024, 2048, 3072, 4096
COL_IQ, COL_KPE, COL_IKW, D_IN_F32 = 5120, 6144, 6272, 6400
COL16_DV, COL16_FV, COL16_G, D_IN_BF16 = 0, 1024, 2048, 8192


def _vmem_limit(block_bytes, scratch_bytes=0):
    need = 2 * sum(block_bytes) + scratch_bytes + VMEM_COMPILER_RESERVE
    return int(min(need, V7X_VMEM_BYTES - 2 * 1024 * 1024))


def _nbytes(shape, dtype):
    return math.prod(shape) * jnp.dtype(dtype).itemsize


def _params(sem, block_bytes, scratch_bytes=0):
    return pltpu.CompilerParams(dimension_semantics=sem,
                                vmem_limit_bytes=_vmem_limit(block_bytes, scratch_bytes))


def _rms(x, g):
    ms = jnp.mean(x * x, axis=-1, keepdims=True)
    return x * lax.rsqrt(ms + EPS) * g


def _sigmoid(x):
    return 1.0 / (1.0 + jnp.exp(-x))


def _norm_matmul_kernel(x_ref, g_ref, w_ref, o_ref, xn_ref):
    @pl.when(pl.program_id(1) == 0)
    def _():
        xn_ref[...] = _rms(x_ref[...], g_ref[...]).astype(BF16)

    o_ref[...] = jnp.dot(xn_ref[...], w_ref[...],
                         preferred_element_type=F32).astype(o_ref.dtype)


def _norm_matmul(x, g, w, tm, tn, out_dtype, name):
    t, d = x.shape
    n = w.shape[1]
    blocks = [_nbytes((tm, d), F32), _nbytes((d, tn), BF16), _nbytes((tm, tn), F32)]
    return pl.pallas_call(
        _norm_matmul_kernel,
        grid=(t // tm, n // tn),
        in_specs=[pl.BlockSpec((tm, d), lambda i, j: (i, 0)),
                  pl.BlockSpec((1, d), lambda i, j: (0, 0)),
                  pl.BlockSpec((d, tn), lambda i, j: (0, j))],
        out_specs=pl.BlockSpec((tm, tn), lambda i, j: (i, j)),
        out_shape=jax.ShapeDtypeStruct((t, n), out_dtype),
        scratch_shapes=[pltpu.VMEM((tm, d), BF16)],
        compiler_params=_params(("parallel", "arbitrary"), blocks,
                                _nbytes((tm, d), BF16) + _nbytes((tm, tn), F32)),
        name=name,
    )(x, g, w)


def _ffn_kernel(h_ref, g_ref, wa_ref, wb_ref, w2_ref, o_ref, xn_ref):
    @pl.when(pl.program_id(1) == 0)
    def _():
        h = h_ref[...]
        xn_ref[...] = _rms(h, g_ref[...]).astype(BF16)
        o_ref[...] = h

    xn = xn_ref[...]
    a = jnp.dot(xn, wa_ref[...], preferred_element_type=F32)
    act = (a * _sigmoid(a)) * jnp.dot(xn, wb_ref[...], preferred_element_type=F32)
    o_ref[...] += jnp.dot(act.astype(BF16), w2_ref[...], preferred_element_type=F32)


def _ffn(h, g, w1, w2, tm, tf):
    t, d = h.shape
    nf = w2.shape[0] // tf
    blocks = [_nbytes((tm, d), F32), 2 * _nbytes((d, tf), BF16), _nbytes((tf, d), BF16),
              _nbytes((tm, d), F32)]
    live = _nbytes((tm, d), BF16) + 2 * _nbytes((tm, tf), F32) + _nbytes((tm, d), F32)
    return pl.pallas_call(
        _ffn_kernel,
        grid=(t // tm, nf),
        in_specs=[pl.BlockSpec((tm, d), lambda i, f: (i, 0)),
                  pl.BlockSpec((1, d), lambda i, f: (0, 0)),
                  pl.BlockSpec((d, tf), lambda i, f: (0, f)),
                  pl.BlockSpec((d, tf), lambda i, f: (0, nf + f)),
                  pl.BlockSpec((tf, d), lambda i, f: (f, 0))],
        out_specs=pl.BlockSpec((tm, d), lambda i, f: (i, 0)),
        out_shape=jax.ShapeDtypeStruct((t, d), F32),
        scratch_shapes=[pltpu.VMEM((tm, d), BF16)],
        compiler_params=_params(("parallel", "arbitrary"), blocks, live),
        name="ffn",
    )(h, g, w1, w1, w2)


def _merge_wo_kernel(h_ref, oa_ref, ob_ref, oc_ref, ga_ref, gb_ref, gc_ref, bg_ref, wbr_ref,
                     wo_ref, o_ref):
    d = h_ref.shape[1]

    def branch(k, o_branch_ref, g_ref):
        gate = _sigmoid(g_ref[...].astype(F32) + bg_ref[:, k * d:(k + 1) * d])
        return gate * jnp.dot(o_branch_ref[...], wbr_ref[k], preferred_element_type=F32)

    merged = branch(0, oa_ref, ga_ref) + branch(1, ob_ref, gb_ref) + branch(2, oc_ref, gc_ref)
    o_ref[...] = h_ref[...] + jnp.dot(merged.astype(BF16), wo_ref[...],
                                      preferred_element_type=F32)


def _merge_wo(h, o_a, o_b, o_c, y16, b_gate, w_branch, w_o, tm, keep=None):
    t, d = h.shape
    gate_block0 = COL16_G // d
    n_blocks = t // tm
    src = lambda i: i
    if keep is not None:
        nq, nxb = keep
        n_blocks = n_blocks // nq * nxb
        src = lambda i: (i // nxb) * nq + i % nxb
    resident = pl.Buffered(1)
    blocks = [2 * _nbytes((tm, d), F32), 3 * _nbytes((tm, BRANCH_WIDTH), BF16),
              3 * _nbytes((tm, d), BF16)]
    weights = _nbytes((N_BRANCH, BRANCH_WIDTH, d), BF16) + _nbytes((d, d), BF16)
    o_spec = pl.BlockSpec((tm, BRANCH_WIDTH), lambda i: (src(i), 0))
    g_spec = lambda k: pl.BlockSpec((tm, d), lambda i: (src(i), gate_block0 + k))
    return pl.pallas_call(
        _merge_wo_kernel,
        grid=(n_blocks,),
        in_specs=[pl.BlockSpec((tm, d), lambda i: (src(i), 0)), o_spec, o_spec, o_spec,
                  g_spec(0), g_spec(1), g_spec(2),
                  pl.BlockSpec((1, N_BRANCH * d), lambda i: (0, 0)),
                  pl.BlockSpec((N_BRANCH, BRANCH_WIDTH, d), lambda i: (0, 0, 0),
                               pipeline_mode=resident),
                  pl.BlockSpec((d, d), lambda i: (0, 0), pipeline_mode=resident)],
        out_specs=pl.BlockSpec((tm, d), lambda i: (i, 0)),
        out_shape=jax.ShapeDtypeStruct((n_blocks * tm, d), F32),
        compiler_params=_params(("parallel",), blocks, weights + 4 * _nbytes((tm, d), F32)),
        name="merge_wo",
    )(h, o_a, o_b, o_c, y16, y16, y16, b_gate, w_branch, w_o)


ROPE_COS, ROPE_SIN, ROPE_C2, ROPE_SA2, ROPE_SB2, ROPE_C1, ROPE_SA1, ROPE_SB1 = (
    k * LANE for k in range(8))
ROPE_TABLE_WIDTH = 8 * LANE


def _tab(rope_ref, off):
    return rope_ref[:, off:off + LANE]


def _rope128(y, rope_ref):
    return y * _tab(rope_ref, ROPE_COS) + pltpu.roll(y, 64, 1) * _tab(rope_ref, ROPE_SIN)


def _rope64(y, rope_ref, c, sa, sb):
    return (y * _tab(rope_ref, c) + pltpu.roll(y, 96, 1) * _tab(rope_ref, sa)
            + pltpu.roll(y, 32, 1) * _tab(rope_ref, sb))


def _norm_rope_heads(x_ref, g, rope_ref, o_ref):
    for h in range(x_ref.shape[1] // HEAD_DIM):
        sl = slice(h * HEAD_DIM, (h + 1) * HEAD_DIM)
        o_ref[:, sl] = _rope128(_rms(x_ref[:, sl], g), rope_ref).astype(BF16)


def _store_transposed(v, vt_ref):
    for b in range(v.shape[1] // LANE):
        sl = slice(b * LANE, (b + 1) * LANE)
        vt_ref[0, sl, :] = v[:, sl].T.astype(BF16)


def _prep_dsa_kernel(dq_ref, dk_ref, dv_ref, iq_ref, ikw_ref, rope_ref, g_ref,
                     qo_ref, ko_ref, vto_ref, iqo_ref, ikzo_ref, iwto_ref):
    _norm_rope_heads(dq_ref, g_ref[0:1, :], rope_ref, qo_ref)
    _norm_rope_heads(dk_ref, g_ref[1:2, :], rope_ref, ko_ref)
    _store_transposed(dv_ref[...].astype(F32), vto_ref)
    for p in range(IDX_HEADS // 2):
        sl = slice(p * LANE, (p + 1) * LANE)
        iqo_ref[:, sl] = _rope64(iq_ref[:, sl], rope_ref, ROPE_C2, ROPE_SA2, ROPE_SB2).astype(BF16)
    x = ikw_ref[...]
    ik = _rope64(x, rope_ref, ROPE_C1, ROPE_SA1, ROPE_SB1)
    ikzo_ref[:, 0:LANE] = ik.astype(BF16)
    ikzo_ref[:, LANE:2 * LANE] = pltpu.roll(ik, 64, 1).astype(BF16)
    w = pltpu.roll(x, 64, 1) * (IDX_HEADS ** -0.5 * IDX_DIM ** -0.5)
    iwto_ref[...] = w.T[0:IDX_HEADS, :]


def _prep_diff_kernel(fq_ref, fk_ref, fv_ref, rope_ref, g_ref, qo_ref, ko_ref, vto_ref):
    _norm_rope_heads(fq_ref, g_ref[0:1, :], rope_ref, qo_ref)
    _norm_rope_heads(fk_ref, g_ref[1:2, :], rope_ref, ko_ref)
    _store_transposed(fv_ref[...].astype(F32), vto_ref)


def _prep_mla_kernel(cq_ref, ckv_ref, kpe_ref, rope_ref, qg_ref, kvg_ref, wq_ref, wkv_ref,
                     gqk_ref, qo_ref, ko_ref, vto_ref):
    inv_qk = 1.0 / MLA_QK
    rope1 = functools.partial(_rope64, rope_ref=rope_ref, c=ROPE_C1, sa=ROPE_SA1, sb=ROPE_SB1)

    def ssq(v):
        return jnp.sum(v * v, axis=-1, keepdims=True)

    cq = _rms(cq_ref[...], qg_ref[...]).astype(BF16)
    q = jnp.dot(cq, wq_ref[...], preferred_element_type=F32)
    gq0, gq1 = gqk_ref[0:1, 0:LANE], gqk_ref[0:1, LANE:2 * LANE]
    gk0, gk1 = gqk_ref[1:2, 0:LANE], gqk_ref[1:2, LANE:2 * LANE]
    for h in range(MLA_HEADS):
        c0 = h * MLA_HEAD_PAD
        b0, b1 = q[:, c0:c0 + LANE], q[:, c0 + LANE:c0 + 2 * LANE]
        r = lax.rsqrt(jnp.sum(b0 * b0 + b1 * b1, axis=-1, keepdims=True) * inv_qk + EPS)
        qo_ref[:, c0:c0 + LANE] = (b0 * r * gq0).astype(BF16)
        qo_ref[:, c0 + LANE:c0 + 2 * LANE] = rope1(b1 * r * gq1).astype(BF16)

    ckv = _rms(ckv_ref[...], kvg_ref[...]).astype(BF16)
    kv = jnp.dot(ckv, wkv_ref[...], preferred_element_type=F32)
    nk = MLA_HEADS * MLA_NOPE
    _store_transposed(kv[:, nk:], vto_ref)
    kp = kpe_ref[...]
    skp = ssq(kp)
    kp_rot = rope1(kp * gk1)
    for h in range(MLA_HEADS):
        c0 = h * MLA_HEAD_PAD
        kn = kv[:, h * MLA_NOPE:(h + 1) * MLA_NOPE]
        r = lax.rsqrt((ssq(kn) + skp) * inv_qk + EPS)
        ko_ref[:, c0:c0 + LANE] = (kn * r * gk0).astype(BF16)
        ko_ref[:, c0 + LANE:c0 + 2 * LANE] = (kp_rot * r).astype(BF16)


def _row_spec(tm, width, col_block):
    return pl.BlockSpec((tm, width), lambda i: (i, col_block))


def _full_spec(shape):
    return pl.BlockSpec(shape, lambda i: (0,) * len(shape))


def _rope_spec(tm, lp):
    nblk = lp // tm
    return pl.BlockSpec((tm, ROPE_TABLE_WIDTH), lambda i: (i % nblk, 0))


def _vt_out(t, width):
    return (pl.BlockSpec((1, width, TK), lambda i: (i, 0, 0)),
            jax.ShapeDtypeStruct((t // TK, width, TK), BF16))


def _prep_all_kernel(cq_ref, ckv_ref, kpe_ref, dq_ref, dk_ref, dv_ref, iq_ref, ikw_ref, fq_ref,
                     fk_ref, fv_ref, rope_ref, qg_ref, kvg_ref, wq_ref, wkv_ref, gqk_ref, gd_ref,
                     gf_ref, qm_ref, km_ref, vtm_ref, qd_ref, kd_ref, vtd_ref, iqo_ref, ikzo_ref,
                     iwto_ref, qf_ref, kf_ref, vtf_ref):
    _prep_mla_kernel(cq_ref, ckv_ref, kpe_ref, rope_ref, qg_ref, kvg_ref, wq_ref, wkv_ref,
                     gqk_ref, qm_ref, km_ref, vtm_ref)
    _prep_dsa_kernel(dq_ref, dk_ref, dv_ref, iq_ref, ikw_ref, rope_ref, gd_ref,
                     qd_ref, kd_ref, vtd_ref, iqo_ref, ikzo_ref, iwto_ref)
    _prep_diff_kernel(fq_ref, fk_ref, fv_ref, rope_ref, gf_ref, qf_ref, kf_ref, vtf_ref)


def _prep_all(y, y16, rope, q_norm_g, kv_norm_g, wq, wkv, gqk, g_dsa, g_diff, lp):
    t = y.shape[0]
    tm = TK
    r = MLA_RANK
    w = BRANCH_WIDTH
    qkw = MLA_HEADS * MLA_HEAD_PAD
    blocks = [2 * _nbytes((tm, r), F32), 2 * _nbytes((tm, LANE), F32), 5 * _nbytes((tm, w), F32),
              2 * _nbytes((tm, w), BF16),
              _nbytes((tm, ROPE_TABLE_WIDTH), F32), _nbytes((r, qkw), BF16),
              _nbytes((r, 2 * w), BF16), 2 * _nbytes((tm, qkw), BF16), 8 * _nbytes((tm, w), BF16),
              _nbytes((tm, 2 * LANE), BF16), _nbytes((IDX_HEADS, tm), F32)]
    scratch = 2 * _nbytes((tm, qkw), F32)
    out_w = lambda width: pl.BlockSpec((tm, width), lambda i: (i, 0))
    tok = lambda width: jax.ShapeDtypeStruct((t, width), BF16)
    vt_spec, vt_shape = _vt_out(t, w)
    return pl.pallas_call(
        _prep_all_kernel,
        grid=(t // tm,),
        in_specs=[_row_spec(tm, r, COL_CQ // r), _row_spec(tm, r, COL_CKV // r),
                  _row_spec(tm, LANE, COL_KPE // LANE),
                  _row_spec(tm, w, COL_DQ // w), _row_spec(tm, w, COL_DK // w),
                  _row_spec(tm, w, COL16_DV // w), _row_spec(tm, w, COL_IQ // w),
                  _row_spec(tm, LANE, COL_IKW // LANE),
                  _row_spec(tm, w, COL_FQ // w), _row_spec(tm, w, COL_FK // w),
                  _row_spec(tm, w, COL16_FV // w), _rope_spec(tm, lp),
                  _full_spec((1, r)), _full_spec((1, r)), _full_spec((r, qkw)),
                  _full_spec((r, 2 * w)), _full_spec((2, MLA_HEAD_PAD)),
                  _full_spec((2, HEAD_DIM)), _full_spec((2, HEAD_DIM))],
        out_specs=[out_w(qkw), out_w(qkw), vt_spec, out_w(w), out_w(w), vt_spec, out_w(w),
                   out_w(2 * LANE), pl.BlockSpec((IDX_HEADS, tm), lambda i: (0, i)),
                   out_w(w), out_w(w), vt_spec],
        out_shape=[tok(qkw), tok(qkw), vt_shape, tok(w), tok(w), vt_shape, tok(w),
                   tok(2 * LANE), jax.ShapeDtypeStruct((IDX_HEADS, t), F32),
                   tok(w), tok(w), vt_shape],
        compiler_params=_params(("parallel",), blocks, scratch),
        name="prep",
    )(y, y, y, y, y, y16, y, y, y, y, y16, rope, q_norm_g, kv_norm_g, wq, wkv, gqk, g_dsa,
      g_diff)


BIAS_META, BIAS_NONE, BIAS_DIAG, BIAS_META_Q = 0, 1, 2, 3
LOG2E = math.log2(math.e)


def _schedule(i, nxb):
    meta_q = i == nxb
    n_steps = jnp.where(meta_q, 1, i + 2)

    def step(c):
        chunk = jnp.where(c == 0, nxb, c - 1)
        kind = jnp.where(meta_q, BIAS_META_Q,
                         jnp.where(c == 0, BIAS_META, jnp.where(c - 1 == i, BIAS_DIAG, BIAS_NONE)))
        return chunk, kind

    return n_steps, step


def _chunk_rows(chunk):
    return pl.ds(pl.multiple_of(chunk * TK, TK), TK)


def _kq(k, q):
    return lax.dot_general(k, q, (((1,), (1,)), ((), ())), preferred_element_type=F32)


def _fold_rows(x, op):
    parts = [x[r:r + 8, :] for r in range(0, x.shape[0], 8)]
    while len(parts) > 1:
        parts = [op(parts[j], parts[j + 1]) for j in range(0, len(parts) - 1, 2)] + (
            [parts[-1]] if len(parts) % 2 else [])
    return parts[0]


def _flash_heads(q_ref, k_ref, vt_ref, heads, scale, n_steps, step, bias_fn, acc_ref, s_ref):
    nh = len(heads)
    scale2 = scale * LOG2E
    acc_ref[...] = jnp.zeros_like(acc_ref)

    def scores(c):
        chunk, kind = step(c)
        rows = _chunk_rows(chunk)
        bias = bias_fn(c, kind)
        for h, (qk_cols, _) in enumerate(heads):
            s_ref[h] = _kq(k_ref[rows, qk_cols], q_ref[:, qk_cols]) * scale2 + bias

    scores(0)

    def body(c, carry):
        ms, ls = carry
        chunk, _ = step(c)
        new_ms, new_ls = [], []
        for h, (_, v_rows) in enumerate(heads):
            s = s_ref[h]
            m_new = jnp.maximum(ms[h], jnp.max(s, axis=0, keepdims=True))
            alpha = jnp.exp2(ms[h] - m_new)
            p = jnp.exp2(s - m_new)
            new_ms.append(m_new)
            new_ls.append(alpha * ls[h] + jnp.sum(p, axis=0, keepdims=True))
            pv = jnp.dot(vt_ref[chunk, v_rows, :], p.astype(BF16), preferred_element_type=F32)
            acc_ref[h] = alpha * acc_ref[h] + pv
        scores(jnp.minimum(c + 1, n_steps - 1))
        return tuple(new_ms), tuple(new_ls)

    init = (tuple(jnp.full((1, TQ), NEG, F32) for _ in range(nh)),
            tuple(jnp.zeros((1, TQ), F32) for _ in range(nh)))
    _, ls = lax.fori_loop(0, n_steps, body, init)
    return [acc_ref[h] / ls[h] for h in range(nh)]


def _mla_attn_kernel(q_ref, k_ref, vt_ref, btab_ref, o_ref, acc_ref, s_ref, *, nxb):
    n_steps, step = _schedule(pl.program_id(1), nxb)
    heads = [(slice(h * MLA_HEAD_PAD, (h + 1) * MLA_HEAD_PAD), slice(h * MLA_V, (h + 1) * MLA_V))
             for h in range(MLA_HEADS)]
    outs = _flash_heads(q_ref, k_ref, vt_ref, heads, MLA_QK ** -0.5, n_steps, step,
                        lambda c, kind: btab_ref[kind], acc_ref, s_ref)
    for h, o in enumerate(outs):
        o_ref[:, h * MLA_V:(h + 1) * MLA_V] = o.T.astype(o_ref.dtype)


def _diff_attn_kernel(q_ref, k_ref, vt_ref, btab_ref, lam_ref, subg_ref, o_ref, acc_ref, s_ref,
                      *, nxb, lambda_init):
    n_steps, step = _schedule(pl.program_id(1), nxb)
    lv = lam_ref[...]
    dot01 = jnp.sum(lv[0:1, :] * lv[1:2, :], axis=-1, keepdims=True)
    dot23 = jnp.sum(lv[2:3, :] * lv[3:4, :], axis=-1, keepdims=True)
    lam = jnp.exp(dot01) - jnp.exp(dot23) + lambda_init
    heads = [(slice(j * HEAD_DIM, (j + 1) * HEAD_DIM),
              slice((j // 2) * DIFF_V, (j // 2 + 1) * DIFF_V)) for j in range(2 * DIFF_HEADS)]
    maps = _flash_heads(q_ref, k_ref, vt_ref, heads, HEAD_DIM ** -0.5, n_steps, step,
                        lambda c, kind: btab_ref[kind], acc_ref, s_ref)
    for h in range(DIFF_HEADS):
        o = (maps[2 * h] - lam * maps[2 * h + 1]).T
        o_ref[:, h * DIFF_V:(h + 1) * DIFF_V] = (
            _rms(o, subg_ref[...]) * (1.0 - lambda_init)).astype(o_ref.dtype)


def _dsa_attn_kernel(q_ref, iq_ref, iwt_ref, k_ref, ikz_ref, vt_ref, btab_ref, o_ref,
                     acc_ref, s_ref, sc_ref, scm_ref, *, nxb, topk):
    n_steps, step = _schedule(pl.program_id(1), nxb)
    kf = float(topk)
    inf = jnp.inf
    seq = nxb * TK

    def index_scores(rows):
        acc = None
        for p in range(IDX_HEADS // 2):
            iq_pair = iq_ref[:, p * LANE:(p + 1) * LANE]
            for e in range(2):
                logits = _kq(ikz_ref[rows, e * LANE:(e + 1) * LANE], iq_pair)
                term = jnp.maximum(logits, 0.0) * iwt_ref[2 * p + e:2 * p + e + 1, :]
                acc = term if acc is None else acc + term
        return acc

    meta_mask = btab_ref[step(0)[1]][0:N_META, :]
    scm_ref[...] = jnp.where(meta_mask == 0.0, index_scores(slice(seq, seq + N_META)), -inf)

    def score_step(c, carry):
        chunk, kind = step(c)
        sc_ref[c] = jnp.where(btab_ref[kind] == 0.0, index_scores(_chunk_rows(chunk)), -inf)
        return carry

    lax.fori_loop(1, n_steps, score_step, 0)

    def reduce_tiles(tile_fn, op, init):
        part = op(init, tile_fn(scm_ref[...]))
        return lax.fori_loop(1, n_steps, lambda c, acc: op(acc, tile_fn(sc_ref[c])), part)

    def count(pred):
        part = reduce_tiles(lambda s: _fold_rows(jnp.where(pred(s), 1.0, 0.0), jnp.add),
                            jnp.add, jnp.zeros((8, TQ), F32))
        return jnp.sum(part, axis=0, keepdims=True)

    def min_where(pred):
        part = reduce_tiles(lambda s: _fold_rows(jnp.where(pred(s), s, inf), jnp.minimum),
                            jnp.minimum, jnp.full((8, TQ), inf, F32))
        return jnp.min(part, axis=0, keepdims=True)

    mx = jnp.max(reduce_tiles(lambda s: _fold_rows(s, jnp.maximum), jnp.maximum,
                              jnp.full((8, TQ), -inf, F32)), axis=0, keepdims=True)
    mn = min_where(lambda s: s > -inf)
    nv = count(lambda s: s > -inf)

    def bisect(_, carry):
        lo, hi = carry
        mid = 0.5 * (lo + hi)
        ge = count(lambda s: s >= mid) >= kf
        return jnp.where(ge, mid, lo), jnp.where(ge, hi, mid)

    lo, _ = lax.fori_loop(0, BISECT_STEPS, bisect, (mn, mx))
    t = min_where(lambda s: s >= lo)
    n_above = count(lambda s: s > t)

    def unsettled(state):
        return jnp.max(jnp.where(state[1] >= kf, 1.0, 0.0)) > 0.0

    def raise_threshold(state):
        t, n_above = state
        t = jnp.where(n_above >= kf, min_where(lambda s: s > t), t)
        return t, count(lambda s: s > t)

    t, n_above = lax.while_loop(unsettled, raise_threshold, (t, n_above))
    keep_all = nv <= kf
    t = jnp.where(keep_all, mn, t)
    n_above = jnp.where(keep_all, count(lambda s: s > mn), n_above)

    n_equal_taken = kf - n_above
    n_equal = count(lambda s: s >= t) - n_above
    pad_rows = jnp.full((TK - N_META, TQ), NEG, F32)

    def store_meta(sel):
        sc_ref[0] = jnp.concatenate([jnp.where(sel, 0.0, NEG), pad_rows], axis=0)

    def mask_plain():
        store_meta(scm_ref[...] >= t)

        def body(c, carry):
            sc_ref[c] = jnp.where(sc_ref[c] >= t, 0.0, NEG)
            return carry

        lax.fori_loop(1, n_steps, body, 0)

    def mask_with_ties():
        lower = (lax.broadcasted_iota(jnp.int32, (TK, TK), 1)
                 <= lax.broadcasted_iota(jnp.int32, (TK, TK), 0)).astype(BF16)

        def select(s, eq_f, prefix, seen):
            return (s > t) | ((eq_f > 0.0) & (seen + prefix <= n_equal_taken))

        s = scm_ref[...]
        eq_f = jnp.where(s == t, 1.0, 0.0)
        eq_pad = jnp.concatenate([eq_f, jnp.zeros((LANE - N_META, TQ), F32)], axis=0)
        prefix = jnp.dot(lower[0:LANE, 0:LANE], eq_pad.astype(BF16),
                         preferred_element_type=F32)[0:N_META, :]
        store_meta(select(s, eq_f, prefix, 0.0))

        def body(c, seen):
            s = sc_ref[c]
            eq_f = jnp.where(s == t, 1.0, 0.0)
            prefix = jnp.dot(lower, eq_f.astype(BF16), preferred_element_type=F32)
            sc_ref[c] = jnp.where(select(s, eq_f, prefix, seen), 0.0, NEG)
            return seen + jnp.sum(eq_f, axis=0, keepdims=True)

        lax.fori_loop(1, n_steps, body, jnp.sum(eq_f, axis=0, keepdims=True))

    surplus_ties = jnp.max(jnp.where(n_equal > n_equal_taken, 1.0, 0.0)) > 0.0
    lax.cond(surplus_ties, mask_with_ties, mask_plain)

    heads = [(slice(h * HEAD_DIM, (h + 1) * HEAD_DIM),) * 2 for h in range(DSA_HEADS)]
    outs = _flash_heads(q_ref, k_ref, vt_ref, heads, HEAD_DIM ** -0.5, n_steps, step,
                        lambda c, kind: sc_ref[c], acc_ref, s_ref)
    for h, o in enumerate(outs):
        o_ref[:, h * HEAD_DIM:(h + 1) * HEAD_DIM] = o.T.astype(o_ref.dtype)


def _attention_call(kernel, name, bsz, nq, lp, q_specs_arrays, k_arrays, vt, extra, extra_specs,
                    out_width, acc_shape, scratch_shapes=()):
    q_arrays = [a for a, _ in q_specs_arrays]
    q_specs = [s for _, s in q_specs_arrays]
    k_specs = [pl.BlockSpec((lp, a.shape[1]), lambda b, i: (b, 0)) for a in k_arrays]
    vt_spec = pl.BlockSpec((nq, vt.shape[1], TK), lambda b, i: (b, 0, 0))
    btab_spec = pl.BlockSpec((4, TK, TQ), lambda b, i: (0, 0, 0))
    scratch = [pltpu.VMEM(acc_shape, F32), pltpu.VMEM((acc_shape[0], TK, TQ), F32)] + list(
        scratch_shapes)
    blocks = ([_nbytes(s.block_shape, a.dtype) for a, s in q_specs_arrays]
              + [_nbytes((lp, a.shape[1]), a.dtype) for a in k_arrays]
              + [_nbytes((nq, vt.shape[1], TK), BF16), _nbytes((4, TK, TQ), F32),
                 _nbytes((TQ, out_width), BF16)])
    scratch_bytes = sum(_nbytes(s.shape, s.dtype) for s in scratch)
    return pl.pallas_call(
        kernel,
        grid=(bsz, nq),
        in_specs=q_specs + k_specs + [vt_spec, btab_spec] + extra_specs,
        out_specs=pl.BlockSpec((TQ, out_width), lambda b, i: (b * nq + i, 0)),
        out_shape=jax.ShapeDtypeStruct((bsz * lp, out_width), BF16),
        scratch_shapes=scratch,
        compiler_params=_params(("parallel", "arbitrary"), blocks, scratch_bytes),
        name=name,
    )(*q_arrays, *k_arrays, vt, *extra)


def _positions(seq):
    lp = seq + TK
    pos = jnp.zeros((lp,), jnp.int32)
    pos = pos.at[:seq].set(N_META + jnp.arange(seq, dtype=jnp.int32))
    return pos.at[seq:seq + N_META].set(jnp.arange(N_META, dtype=jnp.int32))


def _rope_tables(seq):
    pos = _positions(seq).astype(F32)[:, None]

    def cos_sin(d):
        half = d // 2
        inv = ROPE_THETA ** (-jnp.arange(half, dtype=F32) * (2.0 / d))
        ang = pos * inv[None, :]
        return jnp.cos(ang), jnp.sin(ang)

    c, s = cos_sin(HEAD_DIM)
    c2, s2 = cos_sin(IDX_DIM)
    z = jnp.zeros_like(c2)
    cat = lambda *parts: jnp.concatenate(parts, axis=1)
    return cat(cat(c, c), cat(-s, s),
               cat(c2, c2, c2, c2), cat(-s2, z, -s2, z), cat(z, s2, z, s2),
               cat(c2, c2, z, z), cat(-s2, z, z, z), cat(z, s2, z, z))


def _bias_table():
    k = lax.broadcasted_iota(jnp.int32, (TK, TQ), 0)
    q = lax.broadcasted_iota(jnp.int32, (TK, TQ), 1)
    is_meta = k < N_META
    visible = jnp.stack([
        is_meta,
        jnp.ones((TK, TQ), bool),
        k <= q,
        is_meta & ((k <= q) | (q >= N_META)),
    ])
    return jnp.where(visible, 0.0, NEG).astype(F32)


def _pack_w_in(w):
    z = lambda n: jnp.zeros((w.shape[0], n), w.dtype)
    w_f32 = jnp.concatenate([
        w[:, 0:1024],
        w[:, 1088:3136],
        w[:, 5264:7312],
        w[:, 4160:5184],
        w[:, 1024:1088], z(64),
        w[:, 5184:5264], z(48),
    ], axis=1).astype(BF16)
    w_bf16 = jnp.concatenate([
        w[:, 3136:4160],
        w[:, 7312:8336],
        w[:, 8336:14480],
    ], axis=1).astype(BF16)
    return w_f32, w_bf16


def _pack_mla_weights(w_q_up, w_kv_up, qk_g):
    wq = w_q_up.reshape(MLA_RANK, MLA_HEADS, MLA_QK)
    wq = jnp.pad(wq, ((0, 0), (0, 0), (0, MLA_HEAD_PAD - MLA_QK)))
    wq = wq.reshape(MLA_RANK, MLA_HEADS * MLA_HEAD_PAD).astype(BF16)
    wkv = w_kv_up.reshape(MLA_RANK, MLA_HEADS, MLA_NOPE + MLA_V)
    wkv = jnp.concatenate([wkv[:, :, :MLA_NOPE].reshape(MLA_RANK, -1),
                           wkv[:, :, MLA_NOPE:].reshape(MLA_RANK, -1)], axis=1).astype(BF16)
    gqk = jnp.pad(qk_g, ((0, 0), (0, MLA_HEAD_PAD - MLA_QK)))
    return wq, wkv, gqk


def kernel(x, meta, ln1_g, w_in, b_gate, mla_q_norm_g, mla_kv_norm_g, w_mla_q_up, w_mla_kv_up,
           mla_qk_g, dsa_qk_g, diff_qk_g, diff_lambda, diff_subln_g, w_branch, w_o, ln2_g,
           w_ffn_in, w_ffn_out):
    bsz, seq, d = x.shape
    assert d == D_MODEL and seq % TQ == 0
    depth = w_in.shape[0]
    nxb = seq // TQ
    nq = nxb + 1
    lp = seq + TK
    topk = min(TOPK_MAX, seq // 4)
    tm_dense = 1024 if (bsz * lp) % 1024 == 0 else 256
    tf = 512

    h = jnp.concatenate(
        [x, jnp.broadcast_to(meta[None].astype(x.dtype), (bsz, N_META, d)),
         jnp.zeros((bsz, lp - seq - N_META, d), x.dtype)], axis=1).reshape(bsz * lp, d)
    rope = _rope_tables(seq)
    btab = _bias_table()
    row = lambda v: v.reshape(1, -1)
    q_block = lambda a: (a, pl.BlockSpec((TQ, a.shape[1]), lambda b, i: (b * nq + i, 0)))

    for layer in range(depth):
        lambda_init = 0.8 - 0.6 * math.exp(-0.3 * layer)
        w_f32, w_bf16 = _pack_w_in(w_in[layer])
        y = _norm_matmul(h, row(ln1_g[layer]), w_f32, tm_dense, 1280, F32, "inproj_f32")
        y16 = _norm_matmul(h, row(ln1_g[layer]), w_bf16, tm_dense, 1024, BF16, "inproj_bf16")

        wq, wkv, gqk = _pack_mla_weights(w_mla_q_up[layer], w_mla_kv_up[layer], mla_qk_g[layer])
        qm, km, vtm, qd, kd, vtd, iq, ikz, iwt, qf, kf, vtf = _prep_all(
            y, y16, rope, row(mla_q_norm_g[layer]), row(mla_kv_norm_g[layer]), wq, wkv, gqk,
            dsa_qk_g[layer], diff_qk_g[layer], lp)

        o_a = _attention_call(
            functools.partial(_mla_attn_kernel, nxb=nxb), "attn_mla", bsz, nq, lp,
            [q_block(qm)], [km], vtm, [btab], [], BRANCH_WIDTH, (MLA_HEADS, MLA_V, TQ))
        iwt_block = (iwt, pl.BlockSpec((IDX_HEADS, TQ), lambda b, i: (0, b * nq + i)))
        o_b = _attention_call(
            functools.partial(_dsa_attn_kernel, nxb=nxb, topk=topk), "attn_dsa", bsz, nq, lp,
            [q_block(qd), q_block(iq), iwt_block], [kd, ikz], vtd, [btab], [], BRANCH_WIDTH,
            (DSA_HEADS, HEAD_DIM, TQ),
            scratch_shapes=[pltpu.VMEM((nq, TK, TQ), F32), pltpu.VMEM((N_META, TQ), F32)])
        o_c = _attention_call(
            functools.partial(_diff_attn_kernel, nxb=nxb, lambda_init=lambda_init), "attn_diff",
            bsz, nq, lp, [q_block(qf)], [kf], vtf,
            [btab, diff_lambda[layer], row(diff_subln_g[layer])],
            [pl.BlockSpec((4, HEAD_DIM), lambda b, i: (0, 0)),
             pl.BlockSpec((1, DIFF_V), lambda b, i: (0, 0))], BRANCH_WIDTH,
            (2 * DIFF_HEADS, DIFF_V, TQ))

        keep = (nq, nxb) if layer == depth - 1 else None
        h = _merge_wo(h, o_a, o_b, o_c, y16, row(b_gate[layer]), w_branch[layer].astype(BF16),
                      w_o[layer].astype(BF16), TQ, keep)
        h = _ffn(h, row(ln2_g[layer]), w_ffn_in[layer].astype(BF16),
                 w_ffn_out[layer].astype(BF16), tm_dense, tf)

    return h.reshape(bsz, seq, d)
```

```python
import functools
import math

import jax
import jax.numpy as jnp
from jax import lax
from jax.experimental import pallas as pl
from jax.experimental.pallas import tpu as pltpu

F32 = jnp.float32
BF16 = jnp.bfloat16

D_MODEL = 2048
N_META = 16
ROPE_THETA = 10000.0
EPS = 1e-6
TOPK_MAX = 256
MLA_HEADS = 8
MLA_RANK = 512
MLA_NOPE = 128
MLA_ROPE = 64
MLA_V = 128
MLA_QK = MLA_NOPE + MLA_ROPE
DSA_HEADS = 8
IDX_HEADS = 16
IDX_DIM = 64
DIFF_HEADS = 4
DIFF_V = 256
HEAD_DIM = 128
N_BRANCH = 3
BRANCH_WIDTH = 1024
D_FF = 5632

LANE = 128
V7X_VMEM_BYTES = 64 * 1024 * 1024
VMEM_COMPILER_RESERVE = 6 * 1024 * 1024

TQ = 256
TK = 256
MLA_HEAD_PAD = 256
NEG = -1e30
BISECT_STEPS = 16
LOG2E = math.log2(math.e)
Q_SCALE_128 = HEAD_DIM ** -0.5 * LOG2E
Q_SCALE_MLA = MLA_QK ** -0.5 * LOG2E

COL_CQ, COL_CKV, COL_DQ, COL_DK, COL_FQ, COL_FK = 0, 512, 1024, 2048, 3072, 4096
COL_IQ, COL_KPE, COL_IKW, D_IN_F32 = 5120, 6144, 6272, 6400
COL16_DV, COL16_FV, COL16_G, D_IN_BF16 = 0, 1024, 2048, 8192


def _vmem_limit(block_bytes, scratch_bytes=0):
    need = 2 * sum(block_bytes) + scratch_bytes + VMEM_COMPILER_RESERVE
    return int(min(need, V7X_VMEM_BYTES - 2 * 1024 * 1024))


def _nbytes(shape, dtype):
    return math.prod(shape) * jnp.dtype(dtype).itemsize


def _params(sem, block_bytes, scratch_bytes=0):
    return pltpu.CompilerParams(dimension_semantics=sem,
                                vmem_limit_bytes=_vmem_limit(block_bytes, scratch_bytes))


def _rms(x, g):
    ms = jnp.mean(x * x, axis=-1, keepdims=True)
    return x * lax.rsqrt(ms + EPS) * g


def _sigmoid(x):
    return 1.0 / (1.0 + jnp.exp(-x))


def _norm_matmul_kernel(x_ref, g_ref, w_ref, o_ref, xn_ref):
    @pl.when(pl.program_id(1) == 0)
    def _():
        xn_ref[...] = _rms(x_ref[...], g_ref[...]).astype(BF16)

    o_ref[...] = jnp.dot(xn_ref[...], w_ref[...],
                         preferred_element_type=F32).astype(o_ref.dtype)


def _norm_matmul(x, g, w, tm, tn, out_dtype, name):
    t, d = x.shape
    n = w.shape[1]
    blocks = [_nbytes((tm, d), F32), _nbytes((d, tn), BF16), _nbytes((tm, tn), F32)]
    return pl.pallas_call(
        _norm_matmul_kernel,
        grid=(t // tm, n // tn),
        in_specs=[pl.BlockSpec((tm, d), lambda i, j: (i, 0)),
                  pl.BlockSpec((1, d), lambda i, j: (0, 0)),
                  pl.BlockSpec((d, tn), lambda i, j: (0, j))],
        out_specs=pl.BlockSpec((tm, tn), lambda i, j: (i, j)),
        out_shape=jax.ShapeDtypeStruct((t, n), out_dtype),
        scratch_shapes=[pltpu.VMEM((tm, d), BF16)],
        compiler_params=_params(("parallel", "arbitrary"), blocks,
                                _nbytes((tm, d), BF16) + _nbytes((tm, tn), F32)),
        name=name,
    )(x, g, w)


def _ffn_kernel(h_ref, g_ref, wa_ref, wb_ref, w2_ref, o_ref, xn_ref):
    @pl.when(pl.program_id(1) == 0)
    def _():
        h = h_ref[...]
        xn_ref[...] = _rms(h, g_ref[...]).astype(BF16)
        o_ref[...] = h

    xn = xn_ref[...]
    a = jnp.dot(xn, wa_ref[...], preferred_element_type=F32)
    act = (a * _sigmoid(a)) * jnp.dot(xn, wb_ref[...], preferred_element_type=F32)
    o_ref[...] += jnp.dot(act.astype(BF16), w2_ref[...], preferred_element_type=F32)


def _ffn(h, g, w1, w2, tm, tf):
    t, d = h.shape
    nf = w2.shape[0] // tf
    blocks = [_nbytes((tm, d), F32), 2 * _nbytes((d, tf), BF16), _nbytes((tf, d), BF16),
              _nbytes((tm, d), F32)]
    live = _nbytes((tm, d), BF16) + 2 * _nbytes((tm, tf), F32) + _nbytes((tm, d), F32)
    return pl.pallas_call(
        _ffn_kernel,
        grid=(t // tm, nf),
        in_specs=[pl.BlockSpec((tm, d), lambda i, f: (i, 0)),
                  pl.BlockSpec((1, d), lambda i, f: (0, 0)),
                  pl.BlockSpec((d, tf), lambda i, f: (0, f)),
                  pl.BlockSpec((d, tf), lambda i, f: (0, nf + f)),
                  pl.BlockSpec((tf, d), lambda i, f: (f, 0))],
        out_specs=pl.BlockSpec((tm, d), lambda i, f: (i, 0)),
        out_shape=jax.ShapeDtypeStruct((t, d), F32),
        scratch_shapes=[pltpu.VMEM((tm, d), BF16)],
        compiler_params=_params(("parallel", "arbitrary"), blocks, live),
        name="ffn",
    )(h, g, w1, w1, w2)


def _merge_wo_kernel(h_ref, oa_ref, ob_ref, oc_ref, ga_ref, gb_ref, gc_ref, bg_ref, wbr_ref,
                     wo_ref, o_ref):
    d = h_ref.shape[1]

    def branch(k, o_branch_ref, g_ref):
        gate = _sigmoid(g_ref[...].astype(F32) + bg_ref[:, k * d:(k + 1) * d])
        return gate * jnp.dot(o_branch_ref[...], wbr_ref[k], preferred_element_type=F32)

    merged = branch(0, oa_ref, ga_ref) + branch(1, ob_ref, gb_ref) + branch(2, oc_ref, gc_ref)
    o_ref[...] = h_ref[...] + jnp.dot(merged.astype(BF16), wo_ref[...],
                                      preferred_element_type=F32)


def _merge_wo(h, o_a, o_b, o_c, y16, b_gate, w_branch, w_o, tm, keep=None):
    t, d = h.shape
    gate_block0 = COL16_G // d
    n_blocks = t // tm
    src = lambda i: i
    if keep is not None:
        nq, nxb = keep
        n_blocks = n_blocks // nq * nxb
        src = lambda i: (i // nxb) * nq + i % nxb
    resident = pl.Buffered(1)
    blocks = [2 * _nbytes((tm, d), F32), 3 * _nbytes((tm, BRANCH_WIDTH), BF16),
              3 * _nbytes((tm, d), BF16)]
    weights = _nbytes((N_BRANCH, BRANCH_WIDTH, d), BF16) + _nbytes((d, d), BF16)
    o_spec = pl.BlockSpec((tm, BRANCH_WIDTH), lambda i: (src(i), 0))
    g_spec = lambda k: pl.BlockSpec((tm, d), lambda i: (src(i), gate_block0 + k))
    return pl.pallas_call(
        _merge_wo_kernel,
        grid=(n_blocks,),
        in_specs=[pl.BlockSpec((tm, d), lambda i: (src(i), 0)), o_spec, o_spec, o_spec,
                  g_spec(0), g_spec(1), g_spec(2),
                  pl.BlockSpec((1, N_BRANCH * d), lambda i: (0, 0)),
                  pl.BlockSpec((N_BRANCH, BRANCH_WIDTH, d), lambda i: (0, 0, 0),
                               pipeline_mode=resident),
                  pl.BlockSpec((d, d), lambda i: (0, 0), pipeline_mode=resident)],
        out_specs=pl.BlockSpec((tm, d), lambda i: (i, 0)),
        out_shape=jax.ShapeDtypeStruct((n_blocks * tm, d), F32),
        compiler_params=_params(("parallel",), blocks, weights + 4 * _nbytes((tm, d), F32)),
        name="merge_wo",
    )(h, o_a, o_b, o_c, y16, y16, y16, b_gate, w_branch, w_o)


ROPE_COS, ROPE_SIN, ROPE_C2, ROPE_SA2, ROPE_SB2, ROPE_C1, ROPE_SA1, ROPE_SB1 = (
    k * LANE for k in range(8))
ROPE_TABLE_WIDTH = 8 * LANE


def _tab(rope_ref, off):
    return rope_ref[:, off:off + LANE]


def _rope128(y, rope_ref):
    return y * _tab(rope_ref, ROPE_COS) + pltpu.roll(y, 64, 1) * _tab(rope_ref, ROPE_SIN)


def _rope64(y, rope_ref, c, sa, sb):
    return (y * _tab(rope_ref, c) + pltpu.roll(y, 96, 1) * _tab(rope_ref, sa)
            + pltpu.roll(y, 32, 1) * _tab(rope_ref, sb))


def _norm_rope_heads(x_ref, g, rope_ref, o_ref):
    for h in range(x_ref.shape[1] // HEAD_DIM):
        sl = slice(h * HEAD_DIM, (h + 1) * HEAD_DIM)
        o_ref[:, sl] = _rope128(_rms(x_ref[:, sl], g), rope_ref).astype(BF16)


def _store_transposed(v, vt_ref):
    for b in range(v.shape[1] // LANE):
        sl = slice(b * LANE, (b + 1) * LANE)
        vt_ref[0, sl, :] = v[:, sl].T.astype(BF16)


def _prep_dsa_kernel(dq_ref, dk_ref, dv_ref, iq_ref, ikw_ref, rope_ref, g_ref,
                     qo_ref, ko_ref, vto_ref, iqo_ref, ikzo_ref, iwto_ref):
    _norm_rope_heads(dq_ref, g_ref[0:1, :] * Q_SCALE_128, rope_ref, qo_ref)
    _norm_rope_heads(dk_ref, g_ref[1:2, :], rope_ref, ko_ref)
    _store_transposed(dv_ref[...].astype(F32), vto_ref)
    for p in range(IDX_HEADS // 2):
        sl = slice(p * LANE, (p + 1) * LANE)
        iqo_ref[:, sl] = _rope64(iq_ref[:, sl], rope_ref, ROPE_C2, ROPE_SA2, ROPE_SB2).astype(BF16)
    x = ikw_ref[...]
    ik = _rope64(x, rope_ref, ROPE_C1, ROPE_SA1, ROPE_SB1)
    ikzo_ref[:, 0:LANE] = ik.astype(BF16)
    ikzo_ref[:, LANE:2 * LANE] = pltpu.roll(ik, 64, 1).astype(BF16)
    w = pltpu.roll(x, 64, 1) * (IDX_HEADS ** -0.5 * IDX_DIM ** -0.5)
    iwto_ref[...] = w.T[0:IDX_HEADS, :]


def _prep_diff_kernel(fq_ref, fk_ref, fv_ref, rope_ref, g_ref, qo_ref, ko_ref, vto_ref):
    _norm_rope_heads(fq_ref, g_ref[0:1, :] * Q_SCALE_128, rope_ref, qo_ref)
    _norm_rope_heads(fk_ref, g_ref[1:2, :], rope_ref, ko_ref)
    _store_transposed(fv_ref[...].astype(F32), vto_ref)


def _prep_mla_kernel(cq_ref, ckv_ref, kpe_ref, rope_ref, qg_ref, kvg_ref, wq_ref, wkv_ref,
                     gqk_ref, qo_ref, ko_ref, vto_ref):
    inv_qk = 1.0 / MLA_QK
    rope1 = functools.partial(_rope64, rope_ref=rope_ref, c=ROPE_C1, sa=ROPE_SA1, sb=ROPE_SB1)

    def ssq(v):
        return jnp.sum(v * v, axis=-1, keepdims=True)

    cq = _rms(cq_ref[...], qg_ref[...]).astype(BF16)
    q = jnp.dot(cq, wq_ref[...], preferred_element_type=F32)
    gq0 = gqk_ref[0:1, 0:LANE] * Q_SCALE_MLA
    gq1 = gqk_ref[0:1, LANE:2 * LANE] * Q_SCALE_MLA
    gk0, gk1 = gqk_ref[1:2, 0:LANE], gqk_ref[1:2, LANE:2 * LANE]
    for h in range(MLA_HEADS):
        c0 = h * MLA_HEAD_PAD
        b0, b1 = q[:, c0:c0 + LANE], q[:, c0 + LANE:c0 + 2 * LANE]
        r = lax.rsqrt(jnp.sum(b0 * b0 + b1 * b1, axis=-1, keepdims=True) * inv_qk + EPS)
        qo_ref[:, c0:c0 + LANE] = (b0 * r * gq0).astype(BF16)
        qo_ref[:, c0 + LANE:c0 + 2 * LANE] = rope1(b1 * r * gq1).astype(BF16)

    ckv = _rms(ckv_ref[...], kvg_ref[...]).astype(BF16)
    kv = jnp.dot(ckv, wkv_ref[...], preferred_element_type=F32)
    nk = MLA_HEADS * MLA_NOPE
    _store_transposed(kv[:, nk:], vto_ref)
    kp = kpe_ref[...]
    skp = ssq(kp)
    kp_rot = rope1(kp * gk1)
    for h in range(MLA_HEADS):
        c0 = h * MLA_HEAD_PAD
        kn = kv[:, h * MLA_NOPE:(h + 1) * MLA_NOPE]
        r = lax.rsqrt((ssq(kn) + skp) * inv_qk + EPS)
        ko_ref[:, c0:c0 + LANE] = (kn * r * gk0).astype(BF16)
        ko_ref[:, c0 + LANE:c0 + 2 * LANE] = (kp_rot * r).astype(BF16)


def _row_spec(tm, width, col_block):
    return pl.BlockSpec((tm, width), lambda i: (i, col_block))


def _full_spec(shape):
    return pl.BlockSpec(shape, lambda i: (0,) * len(shape))


def _rope_spec(tm, lp):
    nblk = lp // tm
    return pl.BlockSpec((tm, ROPE_TABLE_WIDTH), lambda i: (i % nblk, 0))


def _vt_out(t, width):
    return (pl.BlockSpec((1, width, TK), lambda i: (i, 0, 0)),
            jax.ShapeDtypeStruct((t // TK, width, TK), BF16))


def _prep_all_kernel(cq_ref, ckv_ref, kpe_ref, dq_ref, dk_ref, dv_ref, iq_ref, ikw_ref, fq_ref,
                     fk_ref, fv_ref, rope_ref, qg_ref, kvg_ref, wq_ref, wkv_ref, gqk_ref, gd_ref,
                     gf_ref, qm_ref, km_ref, vtm_ref, qd_ref, kd_ref, vtd_ref, iqo_ref, ikzo_ref,
                     iwto_ref, qf_ref, kf_ref, vtf_ref):
    _prep_mla_kernel(cq_ref, ckv_ref, kpe_ref, rope_ref, qg_ref, kvg_ref, wq_ref, wkv_ref,
                     gqk_ref, qm_ref, km_ref, vtm_ref)
    _prep_dsa_kernel(dq_ref, dk_ref, dv_ref, iq_ref, ikw_ref, rope_ref, gd_ref,
                     qd_ref, kd_ref, vtd_ref, iqo_ref, ikzo_ref, iwto_ref)
    _prep_diff_kernel(fq_ref, fk_ref, fv_ref, rope_ref, gf_ref, qf_ref, kf_ref, vtf_ref)


def _prep_all(y, y16, rope, q_norm_g, kv_norm_g, wq, wkv, gqk, g_dsa, g_diff, lp):
    t = y.shape[0]
    tm = TK
    r = MLA_RANK
    w = BRANCH_WIDTH
    qkw = MLA_HEADS * MLA_HEAD_PAD
    blocks = [2 * _nbytes((tm, r), F32), 2 * _nbytes((tm, LANE), F32), 5 * _nbytes((tm, w), F32),
              2 * _nbytes((tm, w), BF16),
              _nbytes((tm, ROPE_TABLE_WIDTH), F32), _nbytes((r, qkw), BF16),
              _nbytes((r, 2 * w), BF16), 2 * _nbytes((tm, qkw), BF16), 8 * _nbytes((tm, w), BF16),
              _nbytes((tm, 2 * LANE), BF16), _nbytes((IDX_HEADS, tm), F32)]
    scratch = 2 * _nbytes((tm, qkw), F32)
    out_w = lambda width: pl.BlockSpec((tm, width), lambda i: (i, 0))
    tok = lambda width: jax.ShapeDtypeStruct((t, width), BF16)
    vt_spec, vt_shape = _vt_out(t, w)
    return pl.pallas_call(
        _prep_all_kernel,
        grid=(t // tm,),
        in_specs=[_row_spec(tm, r, COL_CQ // r), _row_spec(tm, r, COL_CKV // r),
                  _row_spec(tm, LANE, COL_KPE // LANE),
                  _row_spec(tm, w, COL_DQ // w), _row_spec(tm, w, COL_DK // w),
                  _row_spec(tm, w, COL16_DV // w), _row_spec(tm, w, COL_IQ // w),
                  _row_spec(tm, LANE, COL_IKW // LANE),
                  _row_spec(tm, w, COL_FQ // w), _row_spec(tm, w, COL_FK // w),
                  _row_spec(tm, w, COL16_FV // w), _rope_spec(tm, lp),
                  _full_spec((1, r)), _full_spec((1, r)), _full_spec((r, qkw)),
                  _full_spec((r, 2 * w)), _full_spec((2, MLA_HEAD_PAD)),
                  _full_spec((2, HEAD_DIM)), _full_spec((2, HEAD_DIM))],
        out_specs=[out_w(qkw), out_w(qkw), vt_spec, out_w(w), out_w(w), vt_spec, out_w(w),
                   out_w(2 * LANE), pl.BlockSpec((IDX_HEADS, tm), lambda i: (0, i)),
                   out_w(w), out_w(w), vt_spec],
        out_shape=[tok(qkw), tok(qkw), vt_shape, tok(w), tok(w), vt_shape, tok(w),
                   tok(2 * LANE), jax.ShapeDtypeStruct((IDX_HEADS, t), F32),
                   tok(w), tok(w), vt_shape],
        compiler_params=_params(("parallel",), blocks, scratch),
        name="prep",
    )(y, y, y, y, y, y16, y, y, y, y, y16, rope, q_norm_g, kv_norm_g, wq, wkv, gqk, g_dsa,
      g_diff)


BIAS_META, BIAS_NONE, BIAS_DIAG, BIAS_META_Q = 0, 1, 2, 3


def _schedule(i, nxb):
    meta_q = i == nxb
    n_steps = jnp.where(meta_q, 1, i + 2)

    def step(c):
        chunk = jnp.where(c == 0, nxb, c - 1)
        kind = jnp.where(meta_q, BIAS_META_Q,
                         jnp.where(c == 0, BIAS_META, jnp.where(c - 1 == i, BIAS_DIAG, BIAS_NONE)))
        return chunk, kind

    return n_steps, step


def _chunk_rows(chunk):
    return pl.ds(pl.multiple_of(chunk * TK, TK), TK)


def _kq(k, q):
    return lax.dot_general(k, q, (((1,), (1,)), ((), ())), preferred_element_type=F32)


def _fold_rows(x, op):
    parts = [x[r:r + 8, :] for r in range(0, x.shape[0], 8)]
    while len(parts) > 1:
        parts = [op(parts[j], parts[j + 1]) for j in range(0, len(parts) - 1, 2)] + (
            [parts[-1]] if len(parts) % 2 else [])
    return parts[0]


def _flash_heads(q_ref, k_ref, vt_ref, heads, n_steps, step, bias_fn, acc_ref, s_ref):
    nh = len(heads)
    acc_ref[...] = jnp.zeros_like(acc_ref)
    last = n_steps - 1

    def scores(c, buf):
        chunk, kind = step(c)
        rows = _chunk_rows(chunk)
        bias = bias_fn(c, kind)
        for h, (qk_cols, _) in enumerate(heads):
            s_ref[buf, h] = _kq(k_ref[rows, qk_cols], q_ref[:, qk_cols]) + bias

    def consume(c, buf, carry):
        ms, ls = carry
        chunk, _ = step(c)
        new_ms, new_ls = [], []
        for h, (_, v_rows) in enumerate(heads):
            s = s_ref[buf, h]
            m_new = jnp.maximum(ms[h], jnp.max(s, axis=0, keepdims=True))
            alpha = jnp.exp2(ms[h] - m_new)
            p = jnp.exp2(s - m_new)
            new_ms.append(m_new)
            new_ls.append(alpha * ls[h] + jnp.sum(p, axis=0, keepdims=True))
            pv = jnp.dot(vt_ref[chunk, v_rows, :], p.astype(BF16), preferred_element_type=F32)
            acc_ref[h] = alpha * acc_ref[h] + pv
        return tuple(new_ms), tuple(new_ls)

    scores(0, 0)

    def pair(j, carry):
        c0 = 2 * j
        scores(c0 + 1, 1)
        carry = consume(c0, 0, carry)
        scores(jnp.minimum(c0 + 2, last), 0)
        return consume(c0 + 1, 1, carry)

    init = (tuple(jnp.full((1, TQ), NEG, F32) for _ in range(nh)),
            tuple(jnp.zeros((1, TQ), F32) for _ in range(nh)))
    carry = lax.fori_loop(0, lax.shift_right_logical(n_steps, 1), pair, init)
    _, ls = lax.cond(jnp.bitwise_and(n_steps, 1) == 1,
                     lambda cr: consume(last, 0, cr), lambda cr: cr, carry)
    return [acc_ref[h] / ls[h] for h in range(nh)]


def _mla_attn_kernel(q_ref, k_ref, vt_ref, btab_ref, o_ref, acc_ref, s_ref, *, nxb):
    n_steps, step = _schedule(pl.program_id(1), nxb)
    heads = [(slice(h * MLA_HEAD_PAD, (h + 1) * MLA_HEAD_PAD), slice(h * MLA_V, (h + 1) * MLA_V))
             for h in range(MLA_HEADS)]
    outs = _flash_heads(q_ref, k_ref, vt_ref, heads, n_steps, step,
                        lambda c, kind: btab_ref[kind], acc_ref, s_ref)
    for h, o in enumerate(outs):
        o_ref[:, h * MLA_V:(h + 1) * MLA_V] = o.T.astype(o_ref.dtype)


def _diff_attn_kernel(q_ref, k_ref, vt_ref, btab_ref, lam_ref, subg_ref, o_ref, acc_ref, s_ref,
                      *, nxb, lambda_init):
    n_steps, step = _schedule(pl.program_id(1), nxb)
    lv = lam_ref[...]
    dot01 = jnp.sum(lv[0:1, :] * lv[1:2, :], axis=-1, keepdims=True)
    dot23 = jnp.sum(lv[2:3, :] * lv[3:4, :], axis=-1, keepdims=True)
    lam = jnp.exp(dot01) - jnp.exp(dot23) + lambda_init
    heads = [(slice(j * HEAD_DIM, (j + 1) * HEAD_DIM),
              slice((j // 2) * DIFF_V, (j // 2 + 1) * DIFF_V)) for j in range(2 * DIFF_HEADS)]
    maps = _flash_heads(q_ref, k_ref, vt_ref, heads, n_steps, step,
                        lambda c, kind: btab_ref[kind], acc_ref, s_ref)
    for h in range(DIFF_HEADS):
        o = (maps[2 * h] - lam * maps[2 * h + 1]).T
        o_ref[:, h * DIFF_V:(h + 1) * DIFF_V] = (
            _rms(o, subg_ref[...]) * (1.0 - lambda_init)).astype(o_ref.dtype)


def _dsa_attn_kernel(q_ref, iq_ref, iwt_ref, k_ref, ikz_ref, vt_ref, btab_ref, o_ref,
                     acc_ref, s_ref, sc_ref, scm_ref, *, nxb, topk):
    n_steps, step = _schedule(pl.program_id(1), nxb)
    kf = float(topk)
    inf = jnp.inf
    seq = nxb * TK

    def index_scores(rows):
        acc = None
        for p in range(IDX_HEADS // 2):
            iq_pair = iq_ref[:, p * LANE:(p + 1) * LANE]
            for e in range(2):
                logits = _kq(ikz_ref[rows, e * LANE:(e + 1) * LANE], iq_pair)
                term = jnp.maximum(logits, 0.0) * iwt_ref[2 * p + e:2 * p + e + 1, :]
                acc = term if acc is None else acc + term
        return acc

    meta_mask = btab_ref[step(0)[1]][0:N_META, :]
    scm_ref[...] = jnp.where(meta_mask == 0.0, index_scores(slice(seq, seq + N_META)), -inf)

    def score_step(c, carry):
        chunk, kind = step(c)
        sc_ref[c] = jnp.where(btab_ref[kind] == 0.0, index_scores(_chunk_rows(chunk)), -inf)
        return carry

    lax.fori_loop(1, n_steps, score_step, 0)

    def reduce_tiles(tile_fn, op, init):
        part = op(init, tile_fn(scm_ref[...]))
        return lax.fori_loop(1, n_steps, lambda c, acc: op(acc, tile_fn(sc_ref[c])), part)

    def count(pred):
        part = reduce_tiles(lambda s: _fold_rows(jnp.where(pred(s), 1.0, 0.0), jnp.add),
                            jnp.add, jnp.zeros((8, TQ), F32))
        return jnp.sum(part, axis=0, keepdims=True)

    def min_where(pred):
        part = reduce_tiles(lambda s: _fold_rows(jnp.where(pred(s), s, inf), jnp.minimum),
                            jnp.minimum, jnp.full((8, TQ), inf, F32))
        return jnp.min(part, axis=0, keepdims=True)

    mx = jnp.max(reduce_tiles(lambda s: _fold_rows(s, jnp.maximum), jnp.maximum,
                              jnp.full((8, TQ), -inf, F32)), axis=0, keepdims=True)
    mn = min_where(lambda s: s > -inf)
    nv = count(lambda s: s > -inf)

    def bisect(_, carry):
        lo, hi = carry
        mid = 0.5 * (lo + hi)
        ge = count(lambda s: s >= mid) >= kf
        return jnp.where(ge, mid, lo), jnp.where(ge, hi, mid)

    lo, _ = lax.fori_loop(0, BISECT_STEPS, bisect, (mn, mx))
    t = min_where(lambda s: s >= lo)
    n_above = count(lambda s: s > t)

    def unsettled(state):
        return jnp.max(jnp.where(state[1] >= kf, 1.0, 0.0)) > 0.0

    def raise_threshold(state):
        t, n_above = state
        t = jnp.where(n_above >= kf, min_where(lambda s: s > t), t)
        return t, count(lambda s: s > t)

    t, n_above = lax.while_loop(unsettled, raise_threshold, (t, n_above))
    keep_all = nv <= kf
    t = jnp.where(keep_all, mn, t)
    n_above = jnp.where(keep_all, count(lambda s: s > mn), n_above)

    n_equal_taken = kf - n_above
    n_equal = count(lambda s: s >= t) - n_above
    pad_rows = jnp.full((TK - N_META, TQ), NEG, F32)

    def store_meta(sel):
        sc_ref[0] = jnp.concatenate([jnp.where(sel, 0.0, NEG), pad_rows], axis=0)

    def mask_plain():
        store_meta(scm_ref[...] >= t)

        def body(c, carry):
            sc_ref[c] = jnp.where(sc_ref[c] >= t, 0.0, NEG)
            return carry

        lax.fori_loop(1, n_steps, body, 0)

    def mask_with_ties():
        lower = (lax.broadcasted_iota(jnp.int32, (TK, TK), 1)
                 <= lax.broadcasted_iota(jnp.int32, (TK, TK), 0)).astype(BF16)

        def select(s, eq_f, prefix, seen):
            return (s > t) | ((eq_f > 0.0) & (seen + prefix <= n_equal_taken))

        s = scm_ref[...]
        eq_f = jnp.where(s == t, 1.0, 0.0)
        eq_pad = jnp.concatenate([eq_f, jnp.zeros((LANE - N_META, TQ), F32)], axis=0)
        prefix = jnp.dot(lower[0:LANE, 0:LANE], eq_pad.astype(BF16),
                         preferred_element_type=F32)[0:N_META, :]
        store_meta(select(s, eq_f, prefix, 0.0))

        def body(c, seen):
            s = sc_ref[c]
            eq_f = jnp.where(s == t, 1.0, 0.0)
            prefix = jnp.dot(lower, eq_f.astype(BF16), preferred_element_type=F32)
            sc_ref[c] = jnp.where(select(s, eq_f, prefix, seen), 0.0, NEG)
            return seen + jnp.sum(eq_f, axis=0, keepdims=True)

        lax.fori_loop(1, n_steps, body, jnp.sum(eq_f, axis=0, keepdims=True))

    surplus_ties = jnp.max(jnp.where(n_equal > n_equal_taken, 1.0, 0.0)) > 0.0
    lax.cond(surplus_ties, mask_with_ties, mask_plain)

    heads = [(slice(h * HEAD_DIM, (h + 1) * HEAD_DIM),) * 2 for h in range(DSA_HEADS)]
    outs = _flash_heads(q_ref, k_ref, vt_ref, heads, n_steps, step,
                        lambda c, kind: sc_ref[c], acc_ref, s_ref)
    for h, o in enumerate(outs):
        o_ref[:, h * HEAD_DIM:(h + 1) * HEAD_DIM] = o.T.astype(o_ref.dtype)


def _attention_call(kernel, name, bsz, nq, lp, q_specs_arrays, k_arrays, vt, extra, extra_specs,
                    out_width, acc_shape, scratch_shapes=(), n_q_run=None):
    q_arrays = [a for a, _ in q_specs_arrays]
    q_specs = [s for _, s in q_specs_arrays]
    k_specs = [pl.BlockSpec((lp, a.shape[1]), lambda b, i: (b, 0)) for a in k_arrays]
    vt_spec = pl.BlockSpec((nq, vt.shape[1], TK), lambda b, i: (b, 0, 0))
    btab_spec = pl.BlockSpec((4, TK, TQ), lambda b, i: (0, 0, 0))
    scratch = [pltpu.VMEM(acc_shape, F32), pltpu.VMEM((2, acc_shape[0], TK, TQ), F32)] + list(
        scratch_shapes)
    blocks = ([_nbytes(s.block_shape, a.dtype) for a, s in q_specs_arrays]
              + [_nbytes((lp, a.shape[1]), a.dtype) for a in k_arrays]
              + [_nbytes((nq, vt.shape[1], TK), BF16), _nbytes((4, TK, TQ), F32),
                 _nbytes((TQ, out_width), BF16)])
    scratch_bytes = sum(_nbytes(s.shape, s.dtype) for s in scratch)
    return pl.pallas_call(
        kernel,
        grid=(bsz, nq if n_q_run is None else n_q_run),
        in_specs=q_specs + k_specs + [vt_spec, btab_spec] + extra_specs,
        out_specs=pl.BlockSpec((TQ, out_width), lambda b, i: (b * nq + i, 0)),
        out_shape=jax.ShapeDtypeStruct((bsz * lp, out_width), BF16),
        scratch_shapes=scratch,
        compiler_params=_params(("parallel", "arbitrary"), blocks, scratch_bytes),
        name=name,
    )(*q_arrays, *k_arrays, vt, *extra)


def _positions(seq):
    lp = seq + TK
    pos = jnp.zeros((lp,), jnp.int32)
    pos = pos.at[:seq].set(N_META + jnp.arange(seq, dtype=jnp.int32))
    return pos.at[seq:seq + N_META].set(jnp.arange(N_META, dtype=jnp.int32))


def _rope_tables(seq):
    pos = _positions(seq).astype(F32)[:, None]

    def cos_sin(d):
        half = d // 2
        inv = ROPE_THETA ** (-jnp.arange(half, dtype=F32) * (2.0 / d))
        ang = pos * inv[None, :]
        return jnp.cos(ang), jnp.sin(ang)

    c, s = cos_sin(HEAD_DIM)
    c2, s2 = cos_sin(IDX_DIM)
    z = jnp.zeros_like(c2)
    cat = lambda *parts: jnp.concatenate(parts, axis=1)
    return cat(cat(c, c), cat(-s, s),
               cat(c2, c2, c2, c2), cat(-s2, z, -s2, z), cat(z, s2, z, s2),
               cat(c2, c2, z, z), cat(-s2, z, z, z), cat(z, s2, z, z))


def _bias_table():
    k = lax.broadcasted_iota(jnp.int32, (TK, TQ), 0)
    q = lax.broadcasted_iota(jnp.int32, (TK, TQ), 1)
    is_meta = k < N_META
    visible = jnp.stack([
        is_meta,
        jnp.ones((TK, TQ), bool),
        k <= q,
        is_meta & ((k <= q) | (q >= N_META)),
    ])
    return jnp.where(visible, 0.0, NEG).astype(F32)


def _pack_w_in(w):
    z = lambda n: jnp.zeros((w.shape[0], n), w.dtype)
    w_f32 = jnp.concatenate([
        w[:, 0:1024],
        w[:, 1088:3136],
        w[:, 5264:7312],
        w[:, 4160:5184],
        w[:, 1024:1088], z(64),
        w[:, 5184:5264], z(48),
    ], axis=1).astype(BF16)
    w_bf16 = jnp.concatenate([
        w[:, 3136:4160],
        w[:, 7312:8336],
        w[:, 8336:14480],
    ], axis=1).astype(BF16)
    return w_f32, w_bf16


def _pack_mla_weights(w_q_up, w_kv_up, qk_g):
    wq = w_q_up.reshape(MLA_RANK, MLA_HEADS, MLA_QK)
    wq = jnp.pad(wq, ((0, 0), (0, 0), (0, MLA_HEAD_PAD - MLA_QK)))
    wq = wq.reshape(MLA_RANK, MLA_HEADS * MLA_HEAD_PAD).astype(BF16)
    wkv = w_kv_up.reshape(MLA_RANK, MLA_HEADS, MLA_NOPE + MLA_V)
    wkv = jnp.concatenate([wkv[:, :, :MLA_NOPE].reshape(MLA_RANK, -1),
                           wkv[:, :, MLA_NOPE:].reshape(MLA_RANK, -1)], axis=1).astype(BF16)
    gqk = jnp.pad(qk_g, ((0, 0), (0, MLA_HEAD_PAD - MLA_QK)))
    return wq, wkv, gqk


def kernel(x, meta, ln1_g, w_in, b_gate, mla_q_norm_g, mla_kv_norm_g, w_mla_q_up, w_mla_kv_up,
           mla_qk_g, dsa_qk_g, diff_qk_g, diff_lambda, diff_subln_g, w_branch, w_o, ln2_g,
           w_ffn_in, w_ffn_out):
    bsz, seq, d = x.shape
    assert d == D_MODEL and seq % TQ == 0
    depth = w_in.shape[0]
    nxb = seq // TQ
    nq = nxb + 1
    lp = seq + TK
    topk = min(TOPK_MAX, seq // 4)
    row_tile = lambda rows: next(t for t in (1024, 512, 256) if rows % t == 0)
    tm_dense = row_tile(bsz * lp)
    tf = 512

    h = jnp.concatenate(
        [x, jnp.broadcast_to(meta[None].astype(x.dtype), (bsz, N_META, d)),
         jnp.zeros((bsz, lp - seq - N_META, d), x.dtype)], axis=1).reshape(bsz * lp, d)
    rope = _rope_tables(seq)
    btab = _bias_table()
    row = lambda v: v.reshape(1, -1)
    q_block = lambda a: (a, pl.BlockSpec((TQ, a.shape[1]), lambda b, i: (b * nq + i, 0)))

    for layer in range(depth):
        lambda_init = 0.8 - 0.6 * math.exp(-0.3 * layer)
        w_f32, w_bf16 = _pack_w_in(w_in[layer])
        y = _norm_matmul(h, row(ln1_g[layer]), w_f32, tm_dense, 1280, F32, "inproj_f32")
        y16 = _norm_matmul(h, row(ln1_g[layer]), w_bf16, tm_dense, 1024, BF16, "inproj_bf16")

        wq, wkv, gqk = _pack_mla_weights(w_mla_q_up[layer], w_mla_kv_up[layer], mla_qk_g[layer])
        qm, km, vtm, qd, kd, vtd, iq, ikz, iwt, qf, kf, vtf = _prep_all(
            y, y16, rope, row(mla_q_norm_g[layer]), row(mla_kv_norm_g[layer]), wq, wkv, gqk,
            dsa_qk_g[layer], diff_qk_g[layer], lp)

        last = layer == depth - 1
        n_q_run, keep = (nxb, (nq, nxb)) if last else (nq, None)
        o_a = _attention_call(
            functools.partial(_mla_attn_kernel, nxb=nxb), "attn_mla", bsz, nq, lp,
            [q_block(qm)], [km], vtm, [btab], [], BRANCH_WIDTH, (MLA_HEADS, MLA_V, TQ),
            n_q_run=n_q_run)
        iwt_block = (iwt, pl.BlockSpec((IDX_HEADS, TQ), lambda b, i: (0, b * nq + i)))
        o_b = _attention_call(
            functools.partial(_dsa_attn_kernel, nxb=nxb, topk=topk), "attn_dsa", bsz, nq, lp,
            [q_block(qd), q_block(iq), iwt_block], [kd, ikz], vtd, [btab], [], BRANCH_WIDTH,
            (DSA_HEADS, HEAD_DIM, TQ),
            scratch_shapes=[pltpu.VMEM((nq, TK, TQ), F32), pltpu.VMEM((N_META, TQ), F32)],
            n_q_run=n_q_run)
        o_c = _attention_call(
            functools.partial(_diff_attn_kernel, nxb=nxb, lambda_init=lambda_init), "attn_diff",
            bsz, nq, lp, [q_block(qf)], [kf], vtf,
            [btab, diff_lambda[layer], row(diff_subln_g[layer])],
            [pl.BlockSpec((4, HEAD_DIM), lambda b, i: (0, 0)),
             pl.BlockSpec((1, DIFF_V), lambda b, i: (0, 0))], BRANCH_WIDTH,
            (2 * DIFF_HEADS, DIFF_V, TQ), n_q_run=n_q_run)

        h = _merge_wo(h, o_a, o_b, o_c, y16, row(b_gate[layer]), w_branch[layer].astype(BF16),
                      w_o[layer].astype(BF16), TQ, keep)
        h = _ffn(h, row(ln2_g[layer]), w_ffn_in[layer].astype(BF16),
                 w_ffn_out[layer].astype(BF16), row_tile(h.shape[0]), tf)

    return h.reshape(bsz, seq, d)
```

```python
import functools
import math

import jax
import jax.numpy as jnp
from jax import lax
from jax.experimental import pallas as pl
from jax.experimental.pallas import tpu as pltpu

F32 = jnp.float32
BF16 = jnp.bfloat16

D_MODEL = 2048
N_META = 16
ROPE_THETA = 10000.0
EPS = 1e-6
TOPK_MAX = 256
MLA_HEADS = 8
MLA_RANK = 512
MLA_NOPE = 128
MLA_ROPE = 64
MLA_V = 128
MLA_QK = MLA_NOPE + MLA_ROPE
DSA_HEADS = 8
IDX_HEADS = 16
IDX_DIM = 64
DIFF_HEADS = 4
DIFF_V = 256
HEAD_DIM = 128
N_BRANCH = 3
BRANCH_WIDTH = 1024
D_FF = 5632

LANE = 128
V7X_VMEM_BYTES = 64 * 1024 * 1024
VMEM_COMPILER_RESERVE = 6 * 1024 * 1024

TQ = 256
TK = 256
MLA_HEAD_PAD = 256
NEG = -1e30
BISECT_STEPS = 16
LOG2E = math.log2(math.e)
Q_SCALE_128 = HEAD_DIM ** -0.5 * LOG2E
Q_SCALE_MLA = MLA_QK ** -0.5 * LOG2E

COL_CQ, COL_CKV, COL_DQ, COL_DK, COL_FQ, COL_FK = 0, 512, 1024, 2048, 3072, 4096
COL_IQ, COL_KPE, COL_IKW, D_IN_F32 = 5120, 6144, 6272, 6400
COL16_DV, COL16_FV, COL16_G, D_IN_BF16 = 0, 1024, 2048, 8192


def _vmem_limit(block_bytes, scratch_bytes=0):
    need = 2 * sum(block_bytes) + scratch_bytes + VMEM_COMPILER_RESERVE
    return int(min(need, V7X_VMEM_BYTES - 2 * 1024 * 1024))


def _nbytes(shape, dtype):
    return math.prod(shape) * jnp.dtype(dtype).itemsize


def _params(sem, block_bytes, scratch_bytes=0):
    return pltpu.CompilerParams(dimension_semantics=sem,
                                vmem_limit_bytes=_vmem_limit(block_bytes, scratch_bytes))


def _rms(x, g):
    ms = jnp.mean(x * x, axis=-1, keepdims=True)
    return x * lax.rsqrt(ms + EPS) * g


def _sigmoid(x):
    return 1.0 / (1.0 + jnp.exp(-x))


def _norm_matmul_kernel(x_ref, g_ref, w_ref, o_ref, xn_ref):
    @pl.when(pl.program_id(1) == 0)
    def _():
        xn_ref[...] = _rms(x_ref[...], g_ref[...]).astype(BF16)

    o_ref[...] = jnp.dot(xn_ref[...], w_ref[...],
                         preferred_element_type=F32).astype(o_ref.dtype)


def _norm_matmul(x, g, w, tm, tn, out_dtype, name):
    t, d = x.shape
    n = w.shape[1]
    blocks = [_nbytes((tm, d), F32), _nbytes((d, tn), BF16), _nbytes((tm, tn), F32)]
    return pl.pallas_call(
        _norm_matmul_kernel,
        grid=(t // tm, n // tn),
        in_specs=[pl.BlockSpec((tm, d), lambda i, j: (i, 0)),
                  pl.BlockSpec((1, d), lambda i, j: (0, 0)),
                  pl.BlockSpec((d, tn), lambda i, j: (0, j))],
        out_specs=pl.BlockSpec((tm, tn), lambda i, j: (i, j)),
        out_shape=jax.ShapeDtypeStruct((t, n), out_dtype),
        scratch_shapes=[pltpu.VMEM((tm, d), BF16)],
        compiler_params=_params(("parallel", "arbitrary"), blocks,
                                _nbytes((tm, d), BF16) + _nbytes((tm, tn), F32)),
        name=name,
    )(x, g, w)


def _ffn_kernel(h_ref, g_ref, wa_ref, wb_ref, w2_ref, o_ref, xn_ref):
    @pl.when(pl.program_id(1) == 0)
    def _():
        h = h_ref[...]
        xn_ref[...] = _rms(h, g_ref[...]).astype(BF16)
        o_ref[...] = h

    xn = xn_ref[...]
    a = jnp.dot(xn, wa_ref[...], preferred_element_type=F32)
    act = (a * _sigmoid(a)) * jnp.dot(xn, wb_ref[...], preferred_element_type=F32)
    o_ref[...] += jnp.dot(act.astype(BF16), w2_ref[...], preferred_element_type=F32)


def _ffn(h, g, w1, w2, tm, tf):
    t, d = h.shape
    nf = w2.shape[0] // tf
    blocks = [_nbytes((tm, d), F32), 2 * _nbytes((d, tf), BF16), _nbytes((tf, d), BF16),
              _nbytes((tm, d), F32)]
    live = _nbytes((tm, d), BF16) + 2 * _nbytes((tm, tf), F32) + _nbytes((tm, d), F32)
    return pl.pallas_call(
        _ffn_kernel,
        grid=(t // tm, nf),
        in_specs=[pl.BlockSpec((tm, d), lambda i, f: (i, 0)),
                  pl.BlockSpec((1, d), lambda i, f: (0, 0)),
                  pl.BlockSpec((d, tf), lambda i, f: (0, f)),
                  pl.BlockSpec((d, tf), lambda i, f: (0, nf + f)),
                  pl.BlockSpec((tf, d), lambda i, f: (f, 0))],
        out_specs=pl.BlockSpec((tm, d), lambda i, f: (i, 0)),
        out_shape=jax.ShapeDtypeStruct((t, d), F32),
        scratch_shapes=[pltpu.VMEM((tm, d), BF16)],
        compiler_params=_params(("parallel", "arbitrary"), blocks, live),
        name="ffn",
    )(h, g, w1, w1, w2)


def _merge_wo_kernel(h_ref, oa_ref, ob_ref, oc_ref, ga_ref, gb_ref, gc_ref, bg_ref, wbr_ref,
                     wo_ref, o_ref):
    d = h_ref.shape[1]

    def branch(k, o_branch_ref, g_ref):
        gate = _sigmoid(g_ref[...].astype(F32) + bg_ref[:, k * d:(k + 1) * d])
        return gate * jnp.dot(o_branch_ref[...], wbr_ref[k], preferred_element_type=F32)

    merged = branch(0, oa_ref, ga_ref) + branch(1, ob_ref, gb_ref) + branch(2, oc_ref, gc_ref)
    o_ref[...] = h_ref[...] + jnp.dot(merged.astype(BF16), wo_ref[...],
                                      preferred_element_type=F32)


def _merge_wo(h, o_a, o_b, o_c, y16, b_gate, w_branch, w_o, tm, keep=None):
    t, d = h.shape
    gate_block0 = COL16_G // d
    n_blocks = t // tm
    src = lambda i: i
    if keep is not None:
        nq, nxb = keep
        n_blocks = n_blocks // nq * nxb
        src = lambda i: (i // nxb) * nq + i % nxb
    resident = pl.Buffered(1)
    blocks = [2 * _nbytes((tm, d), F32), 3 * _nbytes((tm, BRANCH_WIDTH), BF16),
              3 * _nbytes((tm, d), BF16)]
    weights = _nbytes((N_BRANCH, BRANCH_WIDTH, d), BF16) + _nbytes((d, d), BF16)
    o_spec = pl.BlockSpec((tm, BRANCH_WIDTH), lambda i: (i, 0))
    g_spec = lambda k: pl.BlockSpec((tm, d), lambda i: (src(i), gate_block0 + k))
    return pl.pallas_call(
        _merge_wo_kernel,
        grid=(n_blocks,),
        in_specs=[pl.BlockSpec((tm, d), lambda i: (src(i), 0)), o_spec, o_spec, o_spec,
                  g_spec(0), g_spec(1), g_spec(2),
                  pl.BlockSpec((1, N_BRANCH * d), lambda i: (0, 0)),
                  pl.BlockSpec((N_BRANCH, BRANCH_WIDTH, d), lambda i: (0, 0, 0),
                               pipeline_mode=resident),
                  pl.BlockSpec((d, d), lambda i: (0, 0), pipeline_mode=resident)],
        out_specs=pl.BlockSpec((tm, d), lambda i: (i, 0)),
        out_shape=jax.ShapeDtypeStruct((n_blocks * tm, d), F32),
        compiler_params=_params(("parallel",), blocks, weights + 4 * _nbytes((tm, d), F32)),
        name="merge_wo",
    )(h, o_a, o_b, o_c, y16, y16, y16, b_gate, w_branch, w_o)


ROPE_COS, ROPE_SIN, ROPE_C2, ROPE_SA2, ROPE_SB2, ROPE_C1, ROPE_SA1, ROPE_SB1 = (
    k * LANE for k in range(8))
ROPE_TABLE_WIDTH = 8 * LANE


def _tab(rope_ref, off):
    return rope_ref[:, off:off + LANE]


def _rope128(y, rope_ref):
    return y * _tab(rope_ref, ROPE_COS) + pltpu.roll(y, 64, 1) * _tab(rope_ref, ROPE_SIN)


def _rope64(y, rope_ref, c, sa, sb):
    return (y * _tab(rope_ref, c) + pltpu.roll(y, 96, 1) * _tab(rope_ref, sa)
            + pltpu.roll(y, 32, 1) * _tab(rope_ref, sb))


def _norm_rope_heads(x_ref, g, rope_ref, o_ref):
    for h in range(x_ref.shape[1] // HEAD_DIM):
        sl = slice(h * HEAD_DIM, (h + 1) * HEAD_DIM)
        o_ref[:, sl] = _rope128(_rms(x_ref[:, sl], g), rope_ref).astype(BF16)


def _store_transposed(v, vt_ref):
    for b in range(v.shape[1] // LANE):
        sl = slice(b * LANE, (b + 1) * LANE)
        vt_ref[0, sl, :] = v[:, sl].T.astype(BF16)


def _prep_dsa_kernel(dq_ref, dk_ref, dv_ref, iq_ref, ikw_ref, rope_ref, g_ref,
                     qo_ref, ko_ref, vto_ref, iqo_ref, ikzo_ref, iwto_ref):
    _norm_rope_heads(dq_ref, g_ref[0:1, :] * Q_SCALE_128, rope_ref, qo_ref)
    _norm_rope_heads(dk_ref, g_ref[1:2, :], rope_ref, ko_ref)
    _store_transposed(dv_ref[...].astype(F32), vto_ref)
    for p in range(IDX_HEADS // 2):
        sl = slice(p * LANE, (p + 1) * LANE)
        iqo_ref[:, sl] = _rope64(iq_ref[:, sl], rope_ref, ROPE_C2, ROPE_SA2, ROPE_SB2).astype(BF16)
    x = ikw_ref[...]
    ik = _rope64(x, rope_ref, ROPE_C1, ROPE_SA1, ROPE_SB1)
    ikzo_ref[:, 0:LANE] = ik.astype(BF16)
    ikzo_ref[:, LANE:2 * LANE] = pltpu.roll(ik, 64, 1).astype(BF16)
    w = pltpu.roll(x, 64, 1) * (IDX_HEADS ** -0.5 * IDX_DIM ** -0.5)
    iwto_ref[...] = w.T[0:IDX_HEADS, :]


def _prep_diff_kernel(fq_ref, fk_ref, fv_ref, rope_ref, g_ref, qo_ref, ko_ref, vto_ref):
    _norm_rope_heads(fq_ref, g_ref[0:1, :] * Q_SCALE_128, rope_ref, qo_ref)
    _norm_rope_heads(fk_ref, g_ref[1:2, :], rope_ref, ko_ref)
    _store_transposed(fv_ref[...].astype(F32), vto_ref)


def _prep_mla_kernel(cq_ref, ckv_ref, kpe_ref, rope_ref, qg_ref, kvg_ref, wq_ref, wkv_ref,
                     gqk_ref, qo_ref, ko_ref, vto_ref):
    inv_qk = 1.0 / MLA_QK
    rope1 = functools.partial(_rope64, rope_ref=rope_ref, c=ROPE_C1, sa=ROPE_SA1, sb=ROPE_SB1)

    def ssq(v):
        return jnp.sum(v * v, axis=-1, keepdims=True)

    cq = _rms(cq_ref[...], qg_ref[...]).astype(BF16)
    q = jnp.dot(cq, wq_ref[...], preferred_element_type=F32)
    gq0 = gqk_ref[0:1, 0:LANE] * Q_SCALE_MLA
    gq1 = gqk_ref[0:1, LANE:2 * LANE] * Q_SCALE_MLA
    gk0, gk1 = gqk_ref[1:2, 0:LANE], gqk_ref[1:2, LANE:2 * LANE]
    for h in range(MLA_HEADS):
        c0 = h * MLA_HEAD_PAD
        b0, b1 = q[:, c0:c0 + LANE], q[:, c0 + LANE:c0 + 2 * LANE]
        r = lax.rsqrt(jnp.sum(b0 * b0 + b1 * b1, axis=-1, keepdims=True) * inv_qk + EPS)
        qo_ref[:, c0:c0 + LANE] = (b0 * r * gq0).astype(BF16)
        qo_ref[:, c0 + LANE:c0 + 2 * LANE] = rope1(b1 * r * gq1).astype(BF16)

    ckv = _rms(ckv_ref[...], kvg_ref[...]).astype(BF16)
    kv = jnp.dot(ckv, wkv_ref[...], preferred_element_type=F32)
    nk = MLA_HEADS * MLA_NOPE
    _store_transposed(kv[:, nk:], vto_ref)
    kp = kpe_ref[...]
    skp = ssq(kp)
    kp_rot = rope1(kp * gk1)
    for h in range(MLA_HEADS):
        c0 = h * MLA_HEAD_PAD
        kn = kv[:, h * MLA_NOPE:(h + 1) * MLA_NOPE]
        r = lax.rsqrt((ssq(kn) + skp) * inv_qk + EPS)
        ko_ref[:, c0:c0 + LANE] = (kn * r * gk0).astype(BF16)
        ko_ref[:, c0 + LANE:c0 + 2 * LANE] = (kp_rot * r).astype(BF16)


def _row_spec(tm, width, col_block):
    return pl.BlockSpec((tm, width), lambda i: (i, col_block))


def _full_spec(shape):
    return pl.BlockSpec(shape, lambda i: (0,) * len(shape))


def _rope_spec(tm, lp):
    nblk = lp // tm
    return pl.BlockSpec((tm, ROPE_TABLE_WIDTH), lambda i: (i % nblk, 0))


def _vt_out(t, width):
    return (pl.BlockSpec((1, width, TK), lambda i: (i, 0, 0)),
            jax.ShapeDtypeStruct((t // TK, width, TK), BF16))


def _prep_all_kernel(cq_ref, ckv_ref, kpe_ref, dq_ref, dk_ref, dv_ref, iq_ref, ikw_ref, fq_ref,
                     fk_ref, fv_ref, rope_ref, qg_ref, kvg_ref, wq_ref, wkv_ref, gqk_ref, gd_ref,
                     gf_ref, qm_ref, km_ref, vtm_ref, qd_ref, kd_ref, vtd_ref, iqo_ref, ikzo_ref,
                     iwto_ref, qf_ref, kf_ref, vtf_ref):
    _prep_mla_kernel(cq_ref, ckv_ref, kpe_ref, rope_ref, qg_ref, kvg_ref, wq_ref, wkv_ref,
                     gqk_ref, qm_ref, km_ref, vtm_ref)
    _prep_dsa_kernel(dq_ref, dk_ref, dv_ref, iq_ref, ikw_ref, rope_ref, gd_ref,
                     qd_ref, kd_ref, vtd_ref, iqo_ref, ikzo_ref, iwto_ref)
    _prep_diff_kernel(fq_ref, fk_ref, fv_ref, rope_ref, gf_ref, qf_ref, kf_ref, vtf_ref)


def _prep_all(y, y16, rope, q_norm_g, kv_norm_g, wq, wkv, gqk, g_dsa, g_diff, lp):
    t = y.shape[0]
    tm = TK
    r = MLA_RANK
    w = BRANCH_WIDTH
    qkw = MLA_HEADS * MLA_HEAD_PAD
    blocks = [2 * _nbytes((tm, r), F32), 2 * _nbytes((tm, LANE), F32), 5 * _nbytes((tm, w), F32),
              2 * _nbytes((tm, w), BF16),
              _nbytes((tm, ROPE_TABLE_WIDTH), F32), _nbytes((r, qkw), BF16),
              _nbytes((r, 2 * w), BF16), 2 * _nbytes((tm, qkw), BF16), 8 * _nbytes((tm, w), BF16),
              _nbytes((tm, 2 * LANE), BF16), _nbytes((IDX_HEADS, tm), F32)]
    scratch = 2 * _nbytes((tm, qkw), F32)
    out_w = lambda width: pl.BlockSpec((tm, width), lambda i: (i, 0))
    tok = lambda width: jax.ShapeDtypeStruct((t, width), BF16)
    vt_spec, vt_shape = _vt_out(t, w)
    return pl.pallas_call(
        _prep_all_kernel,
        grid=(t // tm,),
        in_specs=[_row_spec(tm, r, COL_CQ // r), _row_spec(tm, r, COL_CKV // r),
                  _row_spec(tm, LANE, COL_KPE // LANE),
                  _row_spec(tm, w, COL_DQ // w), _row_spec(tm, w, COL_DK // w),
                  _row_spec(tm, w, COL16_DV // w), _row_spec(tm, w, COL_IQ // w),
                  _row_spec(tm, LANE, COL_IKW // LANE),
                  _row_spec(tm, w, COL_FQ // w), _row_spec(tm, w, COL_FK // w),
                  _row_spec(tm, w, COL16_FV // w), _rope_spec(tm, lp),
                  _full_spec((1, r)), _full_spec((1, r)), _full_spec((r, qkw)),
                  _full_spec((r, 2 * w)), _full_spec((2, MLA_HEAD_PAD)),
                  _full_spec((2, HEAD_DIM)), _full_spec((2, HEAD_DIM))],
        out_specs=[out_w(qkw), out_w(qkw), vt_spec, out_w(w), out_w(w), vt_spec, out_w(w),
                   out_w(2 * LANE), pl.BlockSpec((IDX_HEADS, tm), lambda i: (0, i)),
                   out_w(w), out_w(w), vt_spec],
        out_shape=[tok(qkw), tok(qkw), vt_shape, tok(w), tok(w), vt_shape, tok(w),
                   tok(2 * LANE), jax.ShapeDtypeStruct((IDX_HEADS, t), F32),
                   tok(w), tok(w), vt_shape],
        compiler_params=_params(("parallel",), blocks, scratch),
        name="prep",
    )(y, y, y, y, y, y16, y, y, y, y, y16, rope, q_norm_g, kv_norm_g, wq, wkv, gqk, g_dsa,
      g_diff)


BIAS_META, BIAS_NONE, BIAS_DIAG, BIAS_META_Q = 0, 1, 2, 3


def _schedule(i, nxb):
    meta_q = i == nxb
    n_steps = jnp.where(meta_q, 1, i + 2)

    def step(c):
        chunk = jnp.where(c == 0, nxb, c - 1)
        kind = jnp.where(meta_q, BIAS_META_Q,
                         jnp.where(c == 0, BIAS_META, jnp.where(c - 1 == i, BIAS_DIAG, BIAS_NONE)))
        return chunk, kind

    return n_steps, step


def _chunk_rows(chunk):
    return pl.ds(pl.multiple_of(chunk * TK, TK), TK)


def _kq(k, q):
    return lax.dot_general(k, q, (((1,), (1,)), ((), ())), preferred_element_type=F32)


def _fold_rows(x, op):
    parts = [x[r:r + 8, :] for r in range(0, x.shape[0], 8)]
    while len(parts) > 1:
        parts = [op(parts[j], parts[j + 1]) for j in range(0, len(parts) - 1, 2)] + (
            [parts[-1]] if len(parts) % 2 else [])
    return parts[0]


def _flash_heads(q_ref, k_ref, vt_ref, heads, n_steps, step, bias_fn, acc_ref, s_ref):
    nh = len(heads)
    acc_ref[...] = jnp.zeros_like(acc_ref)
    last = n_steps - 1

    def scores(c, buf):
        chunk, kind = step(c)
        rows = _chunk_rows(chunk)
        bias = bias_fn(c, kind)
        for h, (qk_cols, _) in enumerate(heads):
            s_ref[buf, h] = _kq(k_ref[rows, qk_cols], q_ref[:, qk_cols]) + bias

    def consume(c, buf, carry):
        ms, ls = carry
        chunk, _ = step(c)
        new_ms, new_ls = [], []
        for h, (_, v_rows) in enumerate(heads):
            s = s_ref[buf, h]
            m_new = jnp.maximum(ms[h], jnp.max(s, axis=0, keepdims=True))
            alpha = jnp.exp2(ms[h] - m_new)
            p = jnp.exp2(s - m_new)
            new_ms.append(m_new)
            new_ls.append(alpha * ls[h] + jnp.sum(p, axis=0, keepdims=True))
            pv = jnp.dot(vt_ref[chunk, v_rows, :], p.astype(BF16), preferred_element_type=F32)
            acc_ref[h] = alpha * acc_ref[h] + pv
        return tuple(new_ms), tuple(new_ls)

    scores(0, 0)

    def pair(j, carry):
        c0 = 2 * j
        scores(c0 + 1, 1)
        carry = consume(c0, 0, carry)
        scores(jnp.minimum(c0 + 2, last), 0)
        return consume(c0 + 1, 1, carry)

    init = (tuple(jnp.full((1, TQ), NEG, F32) for _ in range(nh)),
            tuple(jnp.zeros((1, TQ), F32) for _ in range(nh)))
    carry = lax.fori_loop(0, lax.shift_right_logical(n_steps, 1), pair, init)
    _, ls = lax.cond(jnp.bitwise_and(n_steps, 1) == 1,
                     lambda cr: consume(last, 0, cr), lambda cr: cr, carry)
    return [acc_ref[h] / ls[h] for h in range(nh)]


def _mla_attn_kernel(q_ref, k_ref, vt_ref, btab_ref, o_ref, acc_ref, s_ref, *, nxb):
    n_steps, step = _schedule(pl.program_id(1), nxb)
    heads = [(slice(h * MLA_HEAD_PAD, (h + 1) * MLA_HEAD_PAD), slice(h * MLA_V, (h + 1) * MLA_V))
             for h in range(MLA_HEADS)]
    outs = _flash_heads(q_ref, k_ref, vt_ref, heads, n_steps, step,
                        lambda c, kind: btab_ref[kind], acc_ref, s_ref)
    for h, o in enumerate(outs):
        o_ref[:, h * MLA_V:(h + 1) * MLA_V] = o.T.astype(o_ref.dtype)


def _diff_attn_kernel(q_ref, k_ref, vt_ref, btab_ref, lam_ref, subg_ref, o_ref, acc_ref, s_ref,
                      *, nxb, lambda_init):
    n_steps, step = _schedule(pl.program_id(1), nxb)
    lv = lam_ref[...]
    dot01 = jnp.sum(lv[0:1, :] * lv[1:2, :], axis=-1, keepdims=True)
    dot23 = jnp.sum(lv[2:3, :] * lv[3:4, :], axis=-1, keepdims=True)
    lam = jnp.exp(dot01) - jnp.exp(dot23) + lambda_init
    heads = [(slice(j * HEAD_DIM, (j + 1) * HEAD_DIM),
              slice((j // 2) * DIFF_V, (j // 2 + 1) * DIFF_V)) for j in range(2 * DIFF_HEADS)]
    maps = _flash_heads(q_ref, k_ref, vt_ref, heads, n_steps, step,
                        lambda c, kind: btab_ref[kind], acc_ref, s_ref)
    for h in range(DIFF_HEADS):
        o = (maps[2 * h] - lam * maps[2 * h + 1]).T
        o_ref[:, h * DIFF_V:(h + 1) * DIFF_V] = (
            _rms(o, subg_ref[...]) * (1.0 - lambda_init)).astype(o_ref.dtype)


def _dsa_attn_kernel(q_ref, iq_ref, iwt_ref, k_ref, ikz_ref, vt_ref, btab_ref, o_ref,
                     acc_ref, s_ref, sc_ref, scm_ref, *, nxb, topk):
    n_steps, step = _schedule(pl.program_id(1), nxb)
    kf = float(topk)
    inf = jnp.inf
    seq = nxb * TK

    def index_scores(rows):
        acc = None
        for p in range(IDX_HEADS // 2):
            iq_pair = iq_ref[:, p * LANE:(p + 1) * LANE]
            for e in range(2):
                logits = _kq(ikz_ref[rows, e * LANE:(e + 1) * LANE], iq_pair)
                term = jnp.maximum(logits, 0.0) * iwt_ref[2 * p + e:2 * p + e + 1, :]
                acc = term if acc is None else acc + term
        return acc

    meta_mask = btab_ref[step(0)[1]][0:N_META, :]
    scm_ref[...] = jnp.where(meta_mask == 0.0, index_scores(slice(seq, seq + N_META)), -inf)

    def score_step(c, carry):
        chunk, kind = step(c)
        sc_ref[c] = jnp.where(btab_ref[kind] == 0.0, index_scores(_chunk_rows(chunk)), -inf)
        return carry

    lax.fori_loop(1, n_steps, score_step, 0)

    def reduce_tiles(tile_fn, op, init):
        part = op(init, tile_fn(scm_ref[...]))
        return lax.fori_loop(1, n_steps, lambda c, acc: op(acc, tile_fn(sc_ref[c])), part)

    def count(pred):
        part = reduce_tiles(lambda s: _fold_rows(jnp.where(pred(s), 1.0, 0.0), jnp.add),
                            jnp.add, jnp.zeros((8, TQ), F32))
        return jnp.sum(part, axis=0, keepdims=True)

    def min_where(pred):
        part = reduce_tiles(lambda s: _fold_rows(jnp.where(pred(s), s, inf), jnp.minimum),
                            jnp.minimum, jnp.full((8, TQ), inf, F32))
        return jnp.min(part, axis=0, keepdims=True)

    mx = jnp.max(reduce_tiles(lambda s: _fold_rows(s, jnp.maximum), jnp.maximum,
                              jnp.full((8, TQ), -inf, F32)), axis=0, keepdims=True)
    mn = min_where(lambda s: s > -inf)
    nv = count(lambda s: s > -inf)

    def bisect(_, carry):
        lo, hi = carry
        mid = 0.5 * (lo + hi)
        ge = count(lambda s: s >= mid) >= kf
        return jnp.where(ge, mid, lo), jnp.where(ge, hi, mid)

    lo, _ = lax.fori_loop(0, BISECT_STEPS, bisect, (mn, mx))
    t = min_where(lambda s: s >= lo)
    n_above = count(lambda s: s > t)

    def unsettled(state):
        return jnp.max(jnp.where(state[1] >= kf, 1.0, 0.0)) > 0.0

    def raise_threshold(state):
        t, n_above = state
        t = jnp.where(n_above >= kf, min_where(lambda s: s > t), t)
        return t, count(lambda s: s > t)

    t, n_above = lax.while_loop(unsettled, raise_threshold, (t, n_above))
    keep_all = nv <= kf
    t = jnp.where(keep_all, mn, t)
    n_above = jnp.where(keep_all, count(lambda s: s > mn), n_above)

    n_equal_taken = kf - n_above
    n_equal = count(lambda s: s >= t) - n_above
    pad_rows = jnp.full((TK - N_META, TQ), NEG, F32)

    def store_meta(sel):
        sc_ref[0] = jnp.concatenate([jnp.where(sel, 0.0, NEG), pad_rows], axis=0)

    def mask_plain():
        store_meta(scm_ref[...] >= t)

        def body(c, carry):
            sc_ref[c] = jnp.where(sc_ref[c] >= t, 0.0, NEG)
            return carry

        lax.fori_loop(1, n_steps, body, 0)

    def mask_with_ties():
        lower = (lax.broadcasted_iota(jnp.int32, (TK, TK), 1)
                 <= lax.broadcasted_iota(jnp.int32, (TK, TK), 0)).astype(BF16)

        def select(s, eq_f, prefix, seen):
            return (s > t) | ((eq_f > 0.0) & (seen + prefix <= n_equal_taken))

        s = scm_ref[...]
        eq_f = jnp.where(s == t, 1.0, 0.0)
        eq_pad = jnp.concatenate([eq_f, jnp.zeros((LANE - N_META, TQ), F32)], axis=0)
        prefix = jnp.dot(lower[0:LANE, 0:LANE], eq_pad.astype(BF16),
                         preferred_element_type=F32)[0:N_META, :]
        store_meta(select(s, eq_f, prefix, 0.0))

        def body(c, seen):
            s = sc_ref[c]
            eq_f = jnp.where(s == t, 1.0, 0.0)
            prefix = jnp.dot(lower, eq_f.astype(BF16), preferred_element_type=F32)
            sc_ref[c] = jnp.where(select(s, eq_f, prefix, seen), 0.0, NEG)
            return seen + jnp.sum(eq_f, axis=0, keepdims=True)

        lax.fori_loop(1, n_steps, body, jnp.sum(eq_f, axis=0, keepdims=True))

    surplus_ties = jnp.max(jnp.where(n_equal > n_equal_taken, 1.0, 0.0)) > 0.0
    lax.cond(surplus_ties, mask_with_ties, mask_plain)

    heads = [(slice(h * HEAD_DIM, (h + 1) * HEAD_DIM),) * 2 for h in range(DSA_HEADS)]
    outs = _flash_heads(q_ref, k_ref, vt_ref, heads, n_steps, step,
                        lambda c, kind: sc_ref[c], acc_ref, s_ref)
    for h, o in enumerate(outs):
        o_ref[:, h * HEAD_DIM:(h + 1) * HEAD_DIM] = o.T.astype(o_ref.dtype)


def _attention_call(kernel, name, bsz, nq, lp, q_specs_arrays, k_arrays, vt, extra, extra_specs,
                    out_width, acc_shape, scratch_shapes=(), n_q_run=None):
    n_run = nq if n_q_run is None else n_q_run
    q_arrays = [a for a, _ in q_specs_arrays]
    q_specs = [s for _, s in q_specs_arrays]
    k_specs = [pl.BlockSpec((lp, a.shape[1]), lambda b, i: (b, 0)) for a in k_arrays]
    vt_spec = pl.BlockSpec((nq, vt.shape[1], TK), lambda b, i: (b, 0, 0))
    btab_spec = pl.BlockSpec((4, TK, TQ), lambda b, i: (0, 0, 0))
    scratch = [pltpu.VMEM(acc_shape, F32), pltpu.VMEM((2, acc_shape[0], TK, TQ), F32)] + list(
        scratch_shapes)
    blocks = ([_nbytes(s.block_shape, a.dtype) for a, s in q_specs_arrays]
              + [_nbytes((lp, a.shape[1]), a.dtype) for a in k_arrays]
              + [_nbytes((nq, vt.shape[1], TK), BF16), _nbytes((4, TK, TQ), F32),
                 _nbytes((TQ, out_width), BF16)])
    scratch_bytes = sum(_nbytes(s.shape, s.dtype) for s in scratch)
    return pl.pallas_call(
        kernel,
        grid=(bsz, n_run),
        in_specs=q_specs + k_specs + [vt_spec, btab_spec] + extra_specs,
        out_specs=pl.BlockSpec((TQ, out_width), lambda b, i: (b * n_run + i, 0)),
        out_shape=jax.ShapeDtypeStruct((bsz * n_run * TQ, out_width), BF16),
        scratch_shapes=scratch,
        compiler_params=_params(("parallel", "arbitrary"), blocks, scratch_bytes),
        name=name,
    )(*q_arrays, *k_arrays, vt, *extra)


def _positions(seq):
    lp = seq + TK
    pos = jnp.zeros((lp,), jnp.int32)
    pos = pos.at[:seq].set(N_META + jnp.arange(seq, dtype=jnp.int32))
    return pos.at[seq:seq + N_META].set(jnp.arange(N_META, dtype=jnp.int32))


def _rope_tables(seq):
    pos = _positions(seq).astype(F32)[:, None]

    def cos_sin(d):
        half = d // 2
        inv = ROPE_THETA ** (-jnp.arange(half, dtype=F32) * (2.0 / d))
        ang = pos * inv[None, :]
        return jnp.cos(ang), jnp.sin(ang)

    c, s = cos_sin(HEAD_DIM)
    c2, s2 = cos_sin(IDX_DIM)
    z = jnp.zeros_like(c2)
    cat = lambda *parts: jnp.concatenate(parts, axis=1)
    return cat(cat(c, c), cat(-s, s),
               cat(c2, c2, c2, c2), cat(-s2, z, -s2, z), cat(z, s2, z, s2),
               cat(c2, c2, z, z), cat(-s2, z, z, z), cat(z, s2, z, z))


def _bias_table():
    k = lax.broadcasted_iota(jnp.int32, (TK, TQ), 0)
    q = lax.broadcasted_iota(jnp.int32, (TK, TQ), 1)
    is_meta = k < N_META
    visible = jnp.stack([
        is_meta,
        jnp.ones((TK, TQ), bool),
        k <= q,
        is_meta & ((k <= q) | (q >= N_META)),
    ])
    return jnp.where(visible, 0.0, NEG).astype(F32)


def _pack_w_in(w):
    z = lambda n: jnp.zeros((w.shape[0], n), w.dtype)
    w_f32 = jnp.concatenate([
        w[:, 0:1024],
        w[:, 1088:3136],
        w[:, 5264:7312],
        w[:, 4160:5184],
        w[:, 1024:1088], z(64),
        w[:, 5184:5264], z(48),
    ], axis=1).astype(BF16)
    w_bf16 = jnp.concatenate([
        w[:, 3136:4160],
        w[:, 7312:8336],
        w[:, 8336:14480],
    ], axis=1).astype(BF16)
    return w_f32, w_bf16


def _pack_mla_weights(w_q_up, w_kv_up, qk_g):
    wq = w_q_up.reshape(MLA_RANK, MLA_HEADS, MLA_QK)
    wq = jnp.pad(wq, ((0, 0), (0, 0), (0, MLA_HEAD_PAD - MLA_QK)))
    wq = wq.reshape(MLA_RANK, MLA_HEADS * MLA_HEAD_PAD).astype(BF16)
    wkv = w_kv_up.reshape(MLA_RANK, MLA_HEADS, MLA_NOPE + MLA_V)
    wkv = jnp.concatenate([wkv[:, :, :MLA_NOPE].reshape(MLA_RANK, -1),
                           wkv[:, :, MLA_NOPE:].reshape(MLA_RANK, -1)], axis=1).astype(BF16)
    gqk = jnp.pad(qk_g, ((0, 0), (0, MLA_HEAD_PAD - MLA_QK)))
    return wq, wkv, gqk


def kernel(x, meta, ln1_g, w_in, b_gate, mla_q_norm_g, mla_kv_norm_g, w_mla_q_up, w_mla_kv_up,
           mla_qk_g, dsa_qk_g, diff_qk_g, diff_lambda, diff_subln_g, w_branch, w_o, ln2_g,
           w_ffn_in, w_ffn_out):
    bsz, seq, d = x.shape
    assert d == D_MODEL and seq % TQ == 0
    depth = w_in.shape[0]
    nxb = seq // TQ
    nq = nxb + 1
    lp = seq + TK
    topk = min(TOPK_MAX, seq // 4)
    row_tile = lambda rows: next(t for t in (1024, 512, 256) if rows % t == 0)
    tm_dense = row_tile(bsz * lp)
    tf = 512

    h = jnp.concatenate(
        [x, jnp.broadcast_to(meta[None].astype(x.dtype), (bsz, N_META, d)),
         jnp.zeros((bsz, lp - seq - N_META, d), x.dtype)], axis=1).reshape(bsz * lp, d)
    rope = _rope_tables(seq)
    btab = _bias_table()
    row = lambda v: v.reshape(1, -1)
    q_block = lambda a: (a, pl.BlockSpec((TQ, a.shape[1]), lambda b, i: (b * nq + i, 0)))

    for layer in range(depth):
        lambda_init = 0.8 - 0.6 * math.exp(-0.3 * layer)
        w_f32, w_bf16 = _pack_w_in(w_in[layer])
        y = _norm_matmul(h, row(ln1_g[layer]), w_f32, tm_dense, 1280, F32, "inproj_f32")
        y16 = _norm_matmul(h, row(ln1_g[layer]), w_bf16, tm_dense, 1024, BF16, "inproj_bf16")

        wq, wkv, gqk = _pack_mla_weights(w_mla_q_up[layer], w_mla_kv_up[layer], mla_qk_g[layer])
        qm, km, vtm, qd, kd, vtd, iq, ikz, iwt, qf, kf, vtf = _prep_all(
            y, y16, rope, row(mla_q_norm_g[layer]), row(mla_kv_norm_g[layer]), wq, wkv, gqk,
            dsa_qk_g[layer], diff_qk_g[layer], lp)

        last = layer == depth - 1
        n_q_run, keep = (nxb, (nq, nxb)) if last else (nq, None)
        o_a = _attention_call(
            functools.partial(_mla_attn_kernel, nxb=nxb), "attn_mla", bsz, nq, lp,
            [q_block(qm)], [km], vtm, [btab], [], BRANCH_WIDTH, (MLA_HEADS, MLA_V, TQ),
            n_q_run=n_q_run)
        iwt_block = (iwt, pl.BlockSpec((IDX_HEADS, TQ), lambda b, i: (0, b * nq + i)))
        o_b = _attention_call(
            functools.partial(_dsa_attn_kernel, nxb=nxb, topk=topk), "attn_dsa", bsz, nq, lp,
            [q_block(qd), q_block(iq), iwt_block], [kd, ikz], vtd, [btab], [], BRANCH_WIDTH,
            (DSA_HEADS, HEAD_DIM, TQ),
            scratch_shapes=[pltpu.VMEM((nq, TK, TQ), F32), pltpu.VMEM((N_META, TQ), F32)],
            n_q_run=n_q_run)
        o_c = _attention_call(
            functools.partial(_diff_attn_kernel, nxb=nxb, lambda_init=lambda_init), "attn_diff",
            bsz, nq, lp, [q_block(qf)], [kf], vtf,
            [btab, diff_lambda[layer], row(diff_subln_g[layer])],
            [pl.BlockSpec((4, HEAD_DIM), lambda b, i: (0, 0)),
             pl.BlockSpec((1, DIFF_V), lambda b, i: (0, 0))], BRANCH_WIDTH,
            (2 * DIFF_HEADS, DIFF_V, TQ), n_q_run=n_q_run)

        h = _merge_wo(h, o_a, o_b, o_c, y16, row(b_gate[layer]), w_branch[layer].astype(BF16),
                      w_o[layer].astype(BF16), TQ, keep)
        h = _ffn(h, row(ln2_g[layer]), w_ffn_in[layer].astype(BF16),
                 w_ffn_out[layer].astype(BF16), row_tile(h.shape[0]), tf)

    return h.reshape(bsz, seq, d)
```
